```python
import math
import jax, jax.numpy as jnp
from jax import lax
import numpy as np

D_MODEL = 1024
BATCH = 8
SEQ = 2048
DEPTH = 1

N_HEADS = 16
N_KV_HEADS = 2
HEAD_DIM = 64
GROUP = N_HEADS // N_KV_HEADS
ATTN_WIDTH = N_HEADS * HEAD_DIM
KV_WIDTH = N_KV_HEADS * HEAD_DIM
WINDOW = 128
BLOCK = 128
NUM_BUCKETS = 32
MAX_DISTANCE = 128
CONV_WIDTH = D_MODEL
CONV_KERNEL = 31
MEM_LEN = 256
MEM_HEADS = 4
MEM_HEAD_DIM = 256
MEM_WIDTH = MEM_HEADS * MEM_HEAD_DIM
N_BRANCHES = 3
SPLIT_SIZES = (ATTN_WIDTH, KV_WIDTH, KV_WIDTH, ATTN_WIDTH,
               2 * CONV_WIDTH, CONV_WIDTH,
               MEM_WIDTH, MEM_WIDTH,
               N_BRANCHES * D_MODEL)
IN_WIDTH = sum(SPLIT_SIZES)
SPLIT_POINTS = tuple(int(v) for v in np.cumsum(SPLIT_SIZES)[:-1])
RMS_EPS = 1e-6
LN_EPS = 1e-5

kernel_name = "hybrid_swa_conformer_memxattn_gated_block"


def rms_norm(x, g):
    xf = x.astype(jnp.float32)
    y = xf * lax.rsqrt(jnp.mean(xf * xf, axis=-1, keepdims=True) + RMS_EPS)
    return (y * g.astype(jnp.float32)).astype(x.dtype)


def layer_norm(x, g, b):
    xf = x.astype(jnp.float32)
    mu = jnp.mean(xf, axis=-1, keepdims=True)
    var = jnp.mean(jnp.square(xf - mu), axis=-1, keepdims=True)
    y = (xf - mu) * lax.rsqrt(var + LN_EPS)
    return (y * g.astype(jnp.float32) + b.astype(jnp.float32)).astype(x.dtype)


def t5_bucket(rel):
    n = jnp.maximum(rel, 0)
    max_exact = NUM_BUCKETS // 2
    nf = jnp.maximum(n, 1).astype(jnp.float32)
    large = max_exact + (jnp.log(nf / max_exact) / math.log(MAX_DISTANCE / max_exact)
                         * (NUM_BUCKETS - max_exact)).astype(jnp.int32)
    large = jnp.minimum(large, NUM_BUCKETS - 1)
    return jnp.where(n < max_exact, n, large)


def sliding_window_attention(q, k, v, sinks, rel_bias):
    B, S = q.shape[0], q.shape[1]
    nb = S // BLOCK
    qb = q.reshape(B, nb, BLOCK, N_KV_HEADS, GROUP, HEAD_DIM).astype(jnp.float32)

    def band(t):
        tb = t.reshape(B, nb, BLOCK, N_KV_HEADS, HEAD_DIM)
        prev = jnp.pad(tb, ((0, 0), (1, 0), (0, 0), (0, 0), (0, 0)))[:, :-1]
        return jnp.concatenate([prev, tb], axis=2)

    kb, vb = band(k), band(v)
    s = jnp.einsum('bnqkgd,bnjkd->bnkgqj', qb, kb.astype(jnp.float32)) * (HEAD_DIM ** -0.5)

    qi = jnp.arange(BLOCK)[:, None] + BLOCK
    kj = jnp.arange(2 * BLOCK)[None, :]
    rel = qi - kj
    bias = rel_bias.astype(jnp.float32)[t5_bucket(rel)]
    bias = bias.reshape(BLOCK, 2 * BLOCK, N_KV_HEADS, GROUP).transpose(2, 3, 0, 1)
    s = s + bias

    key_abs = jnp.arange(nb)[:, None, None] * BLOCK - BLOCK + kj[None]
    mask = ((rel >= 0) & (rel < WINDOW))[None] & (key_abs >= 0)
    s = jnp.where(mask[None, :, None, None], s, -1e30)

    sink = jnp.broadcast_to(sinks.astype(jnp.float32).reshape(N_KV_HEADS, GROUP)[None, None, :, :, None, None],
                            s.shape[:-1] + (1,))
    p = jax.nn.softmax(jnp.concatenate([s, sink], axis=-1), axis=-1)[..., :-1]
    o = jnp.einsum('bnkgqj,bnjkd->bnqkgd', p.astype(v.dtype), vb)
    return o.reshape(B, S, ATTN_WIDTH)


def conformer_conv(u, conv_w, conv_b, ln_g, ln_b):
    a, b = jnp.split(u, 2, axis=-1)
    g = a * jax.nn.sigmoid(b)
    y = lax.conv_general_dilated(g, conv_w[:, None, :].astype(g.dtype), window_strides=(1,),
                                 padding=[(CONV_KERNEL - 1, 0)],
                                 dimension_numbers=('NWC', 'WIO', 'NWC'),
                                 feature_group_count=CONV_WIDTH)
    y = y + conv_b
    return jax.nn.silu(layer_norm(y, ln_g, ln_b))


def memory_attention(q, mem_k, mem_v):
    s = jnp.einsum('bshd,bmhd->bhsm', q.astype(jnp.float32), mem_k.astype(jnp.float32)) * (MEM_HEAD_DIM ** -0.5)
    p = jax.nn.softmax(s, axis=-1)
    o = jnp.einsum('bhsm,bmhd->bshd', p.astype(mem_v.dtype), mem_v)
    return o.reshape(q.shape[0], q.shape[1], MEM_WIDTH)


def setup_inputs(seed: int = 0) -> dict:
    key = jax.random.key(seed)
    ks = jax.random.split(key, 16)
    f32 = jnp.float32
    n = lambda k, shape, scale: (jax.random.normal(k, shape, f32) * scale)
    return {
        "x": n(ks[0], (BATCH, SEQ, D_MODEL), 1.0),
        "mem": n(ks[1], (BATCH, MEM_LEN, D_MODEL), 1.0),
        "rel_bias": n(ks[2], (NUM_BUCKETS, N_HEADS), 0.5),
        "norm_g": 1.0 + n(ks[3], (DEPTH, D_MODEL), 0.1),
        "w_in": n(ks[4], (DEPTH, D_MODEL, IN_WIDTH), D_MODEL ** -0.5),
        "sinks": n(ks[5], (DEPTH, N_HEADS), 0.5),
        "conv_w": n(ks[6], (DEPTH, CONV_KERNEL, CONV_WIDTH), CONV_KERNEL ** -0.5),
        "conv_b": n(ks[7], (DEPTH, CONV_WIDTH), 0.02),
        "conv_ln_g": 1.0 + n(ks[8], (DEPTH, CONV_WIDTH), 0.1),
        "conv_ln_b": n(ks[9], (DEPTH, CONV_WIDTH), 0.02),
        "mem_norm_g": 1.0 + n(ks[10], (DEPTH, D_MODEL), 0.1),
        "w_mem_kv": n(ks[11], (DEPTH, D_MODEL, 2 * MEM_WIDTH), D_MODEL ** -0.5),
        "w_branch": n(ks[12], (DEPTH, N_BRANCHES, D_MODEL, D_MODEL), D_MODEL ** -0.5),
        "w_out": n(ks[13], (DEPTH, D_MODEL, D_MODEL), D_MODEL ** -0.5),
        "final_norm_g": 1.0 + n(ks[14], (D_MODEL,), 0.1),
    }


def reference(x, mem, rel_bias, norm_g, w_in, sinks, conv_w, conv_b, conv_ln_g, conv_ln_b,
              mem_norm_g, w_mem_kv, w_branch, w_out, final_norm_g):
    B, S = x.shape[0], x.shape[1]
    for l in range(DEPTH):
        h = rms_norm(x, norm_g[l])
        proj = jnp.einsum('bsd,de->bse', h, w_in[l])
        q_a, k_a, v_a, z_a, u_c, z_c, q_m, z_m, g = jnp.split(proj, SPLIT_POINTS, axis=-1)

        o_a = sliding_window_attention(q_a.reshape(B, S, N_HEADS, HEAD_DIM),
                                       k_a.reshape(B, S, N_KV_HEADS, HEAD_DIM),
                                       v_a.reshape(B, S, N_KV_HEADS, HEAD_DIM),
                                       sinks[l], rel_bias) * jax.nn.silu(z_a)

        o_c = conformer_conv(u_c, conv_w[l], conv_b[l], conv_ln_g[l], conv_ln_b[l]) * jax.nn.silu(z_c)

        memn = rms_norm(mem, mem_norm_g[l])
        mk, mv = jnp.split(jnp.einsum('bmd,de->bme', memn, w_mem_kv[l]), 2, axis=-1)
        o_m = memory_attention(q_m.reshape(B, S, MEM_HEADS, MEM_HEAD_DIM),
                               mk.reshape(B, MEM_LEN, MEM_HEADS, MEM_HEAD_DIM),
                               mv.reshape(B, MEM_LEN, MEM_HEADS, MEM_HEAD_DIM)) * jax.nn.silu(z_m)

        o = jnp.stack([o_a, o_c, o_m], axis=2)
        y_br = jnp.einsum('bsnc,ncd->bsnd', o, w_branch[l])
        gates = jax.nn.sigmoid(g.reshape(B, S, N_BRANCHES, D_MODEL))
        y = jnp.sum(gates * y_br, axis=2)
        x = x + jnp.einsum('bsd,de->bse', y, w_out[l])
    return rms_norm(x, final_norm_g)
```

```python
import functools
import math

import jax
import jax.numpy as jnp
from jax import lax
from jax.experimental import pallas as pl
from jax.experimental.pallas import tpu as pltpu

F32 = jnp.float32
BF16 = jnp.bfloat16

D_MODEL = 1024
N_HEADS = 16
N_KV_HEADS = 2
HEAD_DIM = 64
GROUP = N_HEADS // N_KV_HEADS
WINDOW = 128
BLOCK = 128
NUM_BUCKETS = 32
MAX_DISTANCE = 128
CONV_KERNEL = 31
MEM_LEN = 256
MEM_HEADS = 4
MEM_HEAD_DIM = 256
RMS_EPS = 1e-6
LN_EPS = 1e-5
NEG = -1e30

COL_QA, COL_ZA, COL_UA, COL_UB, COL_ZC, COL_QM, COL_ZM, COL_G0 = range(8)
KV_TAIL = 4 * 2 * HEAD_DIM
PROJ_WIDTH = 10 * D_MODEL + KV_TAIL
KV_BLOCK_COL = (10 * D_MODEL) // KV_TAIL

HALO = 32
VMEM_LIMIT = 56 * 1024 * 1024


def _resident(shape):
    return pl.BlockSpec(shape, lambda *_: (0,) * len(shape), pipeline_mode=pl.Buffered(1))


def _silu(z):
    return z * jax.nn.sigmoid(z)


def _memkv_kernel(mem_ref, g_ref, w_ref, mk_ref, mv_ref):
    m = mem_ref[0]
    ms = jnp.mean(m * m, axis=-1, keepdims=True)
    mn = (m * lax.rsqrt(ms + RMS_EPS) * g_ref[...]).astype(BF16)
    kv = jnp.dot(mn, w_ref[...], preferred_element_type=F32)
    mk_ref[0] = kv[:, :D_MODEL].astype(BF16)
    mv_ref[0] = kv[:, D_MODEL:].astype(BF16)


def _memkv(mem, g, w):
    nb = mem.shape[0]
    return pl.pallas_call(
        _memkv_kernel,
        grid=(nb,),
        in_specs=[pl.BlockSpec((1, MEM_LEN, D_MODEL), lambda b: (b, 0, 0)),
                  _resident((1, D_MODEL)),
                  _resident((D_MODEL, 2 * D_MODEL))],
        out_specs=[pl.BlockSpec((1, MEM_LEN, D_MODEL), lambda b: (b, 0, 0)),
                   pl.BlockSpec((1, MEM_LEN, D_MODEL), lambda b: (b, 0, 0))],
        out_shape=[jax.ShapeDtypeStruct((nb, MEM_LEN, D_MODEL), BF16)] * 2,
        compiler_params=pltpu.CompilerParams(dimension_semantics=("arbitrary",),
                                             vmem_limit_bytes=VMEM_LIMIT),
        name="memkv",
    )(mem, g, w)


INPROJ_TM = 256
INPROJ_CW = 512


def _inproj_kernel(x_ref, g_ref, w_ref, o_ref, h_ref):
    x = x_ref[...]
    ms = jnp.mean(x * x, axis=-1, keepdims=True)
    h_ref[...] = (x * lax.rsqrt(ms + RMS_EPS) * g_ref[...]).astype(BF16)
    for c in range(PROJ_WIDTH // INPROJ_CW):
        cols = slice(c * INPROJ_CW, (c + 1) * INPROJ_CW)
        o_ref[:, cols] = jnp.dot(h_ref[...], w_ref[:, cols],
                                 preferred_element_type=F32).astype(BF16)


def _inproj(x2, g, w):
    t = x2.shape[0]
    return pl.pallas_call(
        _inproj_kernel,
        grid=(t // INPROJ_TM,),
        in_specs=[pl.BlockSpec((INPROJ_TM, D_MODEL), lambda i: (i, 0)),
                  _resident((1, D_MODEL)),
                  _resident((D_MODEL, PROJ_WIDTH))],
        out_specs=pl.BlockSpec((INPROJ_TM, PROJ_WIDTH), lambda i: (i, 0)),
        out_shape=jax.ShapeDtypeStruct((t, PROJ_WIDTH), BF16),
        scratch_shapes=[pltpu.VMEM((INPROJ_TM, D_MODEL), BF16)],
        compiler_params=pltpu.CompilerParams(dimension_semantics=("arbitrary",),
                                             vmem_limit_bytes=VMEM_LIMIT),
        name="inproj",
    )(x2, g, w)


def _swa_kernel(sinks_ref, bucket_ref, rbt_ref, q_ref, z_ref, kvc_ref, kvp_ref, o_ref, bias_ref):
    b = pl.program_id(0)
    n = pl.program_id(1)

    @pl.when((b == 0) & (n == 0))
    def _():
        bucket = bucket_ref[...]
        rbt = rbt_ref[...]
        r0 = jnp.full((N_HEADS, 2 * BLOCK), NEG, F32)
        for bk in range(NUM_BUCKETS):
            r0 = jnp.where(bucket == bk, rbt[:, bk:bk + 1], r0)
        col = lax.broadcasted_iota(jnp.int32, (BLOCK, 2 * BLOCK), 1)
        for h in range(N_HEADS):
            t = jnp.broadcast_to(r0[h:h + 1, :], (BLOCK, 2 * BLOCK))
            t = pltpu.roll(t, 0, 1, stride=1, stride_axis=0)
            bias_ref[0, h * BLOCK:(h + 1) * BLOCK, :] = t
            bias_ref[1, h * BLOCK:(h + 1) * BLOCK, :] = jnp.where(col < BLOCK, NEG, t)

    first = jnp.where(n == 0, 1, 0)
    lo = lax.broadcasted_iota(jnp.int32, (BLOCK, 2 * HEAD_DIM), 1) < HEAD_DIM
    q = q_ref[...] * jnp.asarray(HEAD_DIM ** -0.5, BF16)
    zero = jnp.zeros((), BF16)
    rows_g = GROUP * BLOCK
    outs = []
    for g in range(N_KV_HEADS):
        kcols = slice(g * 2 * HEAD_DIM, (g + 1) * 2 * HEAD_DIM)
        vcols = slice((N_KV_HEADS + g) * 2 * HEAD_DIM, (N_KV_HEADS + g + 1) * 2 * HEAD_DIM)
        kk = jnp.concatenate([kvp_ref[:, kcols], kvc_ref[:, kcols]], axis=0)
        vv = jnp.concatenate([kvp_ref[:, vcols], kvc_ref[:, vcols]], axis=0)
        parts = []
        for j in range(GROUP // 2):
            pair = g * (GROUP // 2) + j
            qp = q[:, pair * 2 * HEAD_DIM:(pair + 1) * 2 * HEAD_DIM]
            parts.append(jnp.where(lo, qp, zero))
            parts.append(jnp.where(lo, zero, qp))
        qs = jnp.concatenate(parts, axis=0)
        s = lax.dot_general(qs, kk, (((1,), (1,)), ((), ())), preferred_element_type=F32)
        s = s + bias_ref[first, g * rows_g:(g + 1) * rows_g, :]
        sink = jnp.concatenate(
            [jnp.full((BLOCK, 1), sinks_ref[g * GROUP + j], F32) for j in range(GROUP)], axis=0)
        m = jnp.maximum(jnp.max(s, axis=-1, keepdims=True), sink)
        p = jnp.exp(s - m)
        denom = jnp.sum(p, axis=-1, keepdims=True) + jnp.exp(sink - m)
        o = jnp.dot(p.astype(BF16), vv, preferred_element_type=F32)
        o = o * (1.0 / denom)
        for j in range(GROUP // 2):
            oe = o[(2 * j) * BLOCK:(2 * j + 1) * BLOCK]
            oo = o[(2 * j + 1) * BLOCK:(2 * j + 2) * BLOCK]
            outs.append(jnp.where(lo, oe, oo))
    oa = jnp.concatenate(outs, axis=1)
    o_ref[...] = (oa * _silu(z_ref[...].astype(F32))).astype(BF16)


def _swa(proj, sinks, bucket_row, rbt, nb, s):
    t = proj.shape[0]
    nblk = s // BLOCK
    row = lambda b, n: b * nblk + n
    return pl.pallas_call(
        _swa_kernel,
        grid=(nb, nblk),
        in_specs=[pl.BlockSpec(memory_space=pltpu.SMEM),
                  _resident((1, 2 * BLOCK)),
                  _resident((N_HEADS, NUM_BUCKETS)),
                  pl.BlockSpec((BLOCK, D_MODEL), lambda b, n: (row(b, n), COL_QA)),
                  pl.BlockSpec((BLOCK, D_MODEL), lambda b, n: (row(b, n), COL_ZA)),
                  pl.BlockSpec((BLOCK, KV_TAIL), lambda b, n: (row(b, n), KV_BLOCK_COL)),
                  pl.BlockSpec((BLOCK, KV_TAIL),
                               lambda b, n: (b * nblk + jnp.maximum(n - 1, 0), KV_BLOCK_COL))],
        out_specs=pl.BlockSpec((BLOCK, D_MODEL), lambda b, n: (row(b, n), 0)),
        out_shape=jax.ShapeDtypeStruct((t, D_MODEL), BF16),
        scratch_shapes=[pltpu.VMEM((2, N_HEADS * BLOCK, 2 * BLOCK), F32)],
        compiler_params=pltpu.CompilerParams(dimension_semantics=("arbitrary", "arbitrary"),
                                             vmem_limit_bytes=VMEM_LIMIT),
        name="swa",
    )(sinks, bucket_row, rbt, proj, proj, proj, proj)


CONV_TC = 256
CONV_R = 64
LANES = 128


def _conv_kernel(ua_ref, ub_ref, zc_ref, cw_ref, cb_ref, lg_ref, lb_ref, o_ref, gext_ref, y_ref):
    n = pl.program_id(1)

    @pl.when(n == 0)
    def _():
        gext_ref[0:HALO, :] = jnp.zeros((HALO, D_MODEL), F32)

    @pl.when(n > 0)
    def _():
        gext_ref[0:HALO, :] = gext_ref[CONV_TC:CONV_TC + HALO, :]

    gext_ref[HALO:HALO + CONV_TC, :] = ua_ref[...].astype(F32) * jax.nn.sigmoid(ub_ref[...].astype(F32))

    off = HALO - (CONV_KERNEL - 1)
    for c in range(D_MODEL // LANES):
        lanes = slice(c * LANES, (c + 1) * LANES)
        for r in range(CONV_TC // CONV_R):
            acc = jnp.zeros((CONV_R, LANES), F32)
            for k in range(CONV_KERNEL):
                start = r * CONV_R + off + k
                acc = acc + cw_ref[k:k + 1, lanes] * gext_ref[start:start + CONV_R, lanes]
            y_ref[r * CONV_R:(r + 1) * CONV_R, lanes] = acc + cb_ref[:, lanes]

    y = y_ref[...]
    mu = jnp.mean(y, axis=-1, keepdims=True)
    yc = y - mu
    var = jnp.mean(yc * yc, axis=-1, keepdims=True)
    ln = yc * lax.rsqrt(var + LN_EPS) * lg_ref[...] + lb_ref[...]
    o_ref[...] = (_silu(ln) * _silu(zc_ref[...].astype(F32))).astype(BF16)


def _conv(proj, cw, cb, lg, lb, nb, s):
    t = proj.shape[0]
    nt = s // CONV_TC
    row = lambda b, n: b * nt + n
    return pl.pallas_call(
        _conv_kernel,
        grid=(nb, nt),
        in_specs=[pl.BlockSpec((CONV_TC, D_MODEL), lambda b, n: (row(b, n), COL_UA)),
                  pl.BlockSpec((CONV_TC, D_MODEL), lambda b, n: (row(b, n), COL_UB)),
                  pl.BlockSpec((CONV_TC, D_MODEL), lambda b, n: (row(b, n), COL_ZC)),
                  _resident((CONV_KERNEL, D_MODEL)),
                  _resident((1, D_MODEL)),
                  _resident((1, D_MODEL)),
                  _resident((1, D_MODEL))],
        out_specs=pl.BlockSpec((CONV_TC, D_MODEL), lambda b, n: (row(b, n), 0)),
        out_shape=jax.ShapeDtypeStruct((t, D_MODEL), BF16),
        scratch_shapes=[pltpu.VMEM((CONV_TC + HALO, D_MODEL), F32),
                        pltpu.VMEM((CONV_TC, D_MODEL), F32)],
        compiler_params=pltpu.CompilerParams(dimension_semantics=("arbitrary", "arbitrary"),
                                             vmem_limit_bytes=VMEM_LIMIT),
        name="conv",
    )(proj, proj, proj, cw, cb, lg, lb)


MERGE_TM = 256


def _merge_kernel(x_ref, qm_ref, zm_ref, g0_ref, g1_ref, g2_ref, oa_ref, oc_ref, mk_ref, mv_ref,
                  wb_ref, wo_ref, fg_ref, out_ref):
    qm = qm_ref[...] * jnp.asarray(MEM_HEAD_DIM ** -0.5, BF16)
    heads = []
    for h in range(MEM_HEADS):
        cols = slice(h * MEM_HEAD_DIM, (h + 1) * MEM_HEAD_DIM)
        s = lax.dot_general(qm[:, cols], mk_ref[0, :, cols], (((1,), (1,)), ((), ())),
                            preferred_element_type=F32)
        m = jnp.max(s, axis=-1, keepdims=True)
        p = jnp.exp(s - m)
        denom = jnp.sum(p, axis=-1, keepdims=True)
        o = jnp.dot(p.astype(BF16), mv_ref[0, :, cols], preferred_element_type=F32)
        heads.append(o * (1.0 / denom))
    om = (jnp.concatenate(heads, axis=1) * _silu(zm_ref[...].astype(F32))).astype(BF16)

    y = jax.nn.sigmoid(g0_ref[...].astype(F32)) * jnp.dot(oa_ref[...], wb_ref[0], preferred_element_type=F32)
    y = y + jax.nn.sigmoid(g1_ref[...].astype(F32)) * jnp.dot(oc_ref[...], wb_ref[1], preferred_element_type=F32)
    y = y + jax.nn.sigmoid(g2_ref[...].astype(F32)) * jnp.dot(om, wb_ref[2], preferred_element_type=F32)
    xo = x_ref[...] + jnp.dot(y.astype(BF16), wo_ref[...], preferred_element_type=F32)
    ms = jnp.mean(xo * xo, axis=-1, keepdims=True)
    out_ref[...] = xo * lax.rsqrt(ms + RMS_EPS) * fg_ref[...]


def _merge(x2, proj, oa, oc, mk, mv, wb, wo, fg, s):
    t = x2.shape[0]
    per_b = s // MERGE_TM
    col = lambda c: pl.BlockSpec((MERGE_TM, D_MODEL), lambda i: (i, c))
    memspec = pl.BlockSpec((1, MEM_LEN, D_MODEL), lambda i: (i // per_b, 0, 0))
    return pl.pallas_call(
        _merge_kernel,
        grid=(t // MERGE_TM,),
        in_specs=[col(0), col(COL_QM), col(COL_ZM), col(COL_G0), col(COL_G0 + 1), col(COL_G0 + 2),
                  col(0), col(0), memspec, memspec,
                  _resident((3, D_MODEL, D_MODEL)),
                  _resident((D_MODEL, D_MODEL)),
                  _resident((1, D_MODEL))],
        out_specs=pl.BlockSpec((MERGE_TM, D_MODEL), lambda i: (i, 0)),
        out_shape=jax.ShapeDtypeStruct((t, D_MODEL), F32),
        compiler_params=pltpu.CompilerParams(dimension_semantics=("arbitrary",),
                                             vmem_limit_bytes=VMEM_LIMIT),
        name="merge",
    )(x2, proj, proj, proj, proj, proj, oa, oc, mk, mv, wb, wo, fg)


def _bucket_row():
    j = jnp.arange(2 * BLOCK)
    d = BLOCK - j
    n = jnp.maximum(d, 0)
    max_exact = NUM_BUCKETS // 2
    nf = jnp.maximum(n, 1).astype(F32)
    large = max_exact + (jnp.log(nf / max_exact) / math.log(MAX_DISTANCE / max_exact)
                         * (NUM_BUCKETS - max_exact)).astype(jnp.int32)
    large = jnp.minimum(large, NUM_BUCKETS - 1)
    bucket = jnp.where(n < max_exact, n, large)
    valid = (d >= 0) & (d < WINDOW)
    return jnp.where(valid, bucket, -1).astype(jnp.int32).reshape(1, 2 * BLOCK)


def kernel(x, mem, rel_bias, norm_g, w_in, sinks, conv_w, conv_b, conv_ln_g, conv_ln_b, mem_norm_g,
           w_mem_kv, w_branch, w_out, final_norm_g):
    nb, s, d = x.shape
    assert d == D_MODEL and w_in.shape[0] == 1 and s % CONV_TC == 0 and s % MERGE_TM == 0
    x2 = x.reshape(nb * s, d)

    w = w_in[0]
    kv = [w[:, D_MODEL + i * HEAD_DIM:D_MODEL + (i + 1) * HEAD_DIM] for i in range(2 * N_KV_HEADS)]
    w_perm = jnp.concatenate(
        [w[:, :D_MODEL], w[:, D_MODEL + 2 * N_KV_HEADS * HEAD_DIM:]] + [kv[i // 2] for i in range(8)],
        axis=1).astype(BF16)

    mk, mv = _memkv(mem, mem_norm_g[0].reshape(1, d), w_mem_kv[0].astype(BF16))
    proj = _inproj(x2, norm_g[0].reshape(1, d), w_perm)
    oa = _swa(proj, sinks[0], _bucket_row(), rel_bias.T, nb, s)
    oc = _conv(proj, conv_w[0], conv_b[0].reshape(1, d), conv_ln_g[0].reshape(1, d),
               conv_ln_b[0].reshape(1, d), nb, s)
    out = _merge(x2, proj, oa, oc, mk, mv, w_branch[0].astype(BF16), w_out[0].astype(BF16),
                 final_norm_g.reshape(1, d), s)
    return out.reshape(nb, s, d)
```

```python
import functools
import math

import jax
import jax.numpy as jnp
from jax import lax
from jax.experimental import pallas as pl
from jax.experimental.pallas import tpu as pltpu

F32 = jnp.float32
BF16 = jnp.bfloat16

D_MODEL = 1024
N_HEADS = 16
N_KV_HEADS = 2
HEAD_DIM = 64
GROUP = N_HEADS // N_KV_HEADS
WINDOW = 128
BLOCK = 128
NUM_BUCKETS = 32
MAX_DISTANCE = 128
CONV_KERNEL = 31
MEM_LEN = 256
MEM_HEADS = 4
MEM_HEAD_DIM = 256
RMS_EPS = 1e-6
LN_EPS = 1e-5
NEG = -1e30

COL_QA, COL_ZA, COL_UA, COL_UB, COL_ZC, COL_QM, COL_ZM, COL_G0 = range(8)
KV_TAIL = 4 * 2 * HEAD_DIM
PROJ_WIDTH = 10 * D_MODEL + KV_TAIL
KV_BLOCK_COL = (10 * D_MODEL) // KV_TAIL

HALO = 32
VMEM_LIMIT = 56 * 1024 * 1024


def _resident(shape):
    return pl.BlockSpec(shape, lambda *_: (0,) * len(shape), pipeline_mode=pl.Buffered(1))


def _silu(z):
    return z * jax.nn.sigmoid(z)


def _memkv_kernel(mem_ref, g_ref, w_ref, mk_ref, mv_ref):
    m = mem_ref[0]
    ms = jnp.mean(m * m, axis=-1, keepdims=True)
    mn = (m * lax.rsqrt(ms + RMS_EPS) * g_ref[...]).astype(BF16)
    kv = jnp.dot(mn, w_ref[...], preferred_element_type=F32)
    mk_ref[0] = kv[:, :D_MODEL].astype(BF16)
    mv_ref[0] = kv[:, D_MODEL:].astype(BF16)


def _memkv(mem, g, w):
    nb = mem.shape[0]
    return pl.pallas_call(
        _memkv_kernel,
        grid=(nb,),
        in_specs=[pl.BlockSpec((1, MEM_LEN, D_MODEL), lambda b: (b, 0, 0)),
                  _resident((1, D_MODEL)),
                  _resident((D_MODEL, 2 * D_MODEL))],
        out_specs=[pl.BlockSpec((1, MEM_LEN, D_MODEL), lambda b: (b, 0, 0)),
                   pl.BlockSpec((1, MEM_LEN, D_MODEL), lambda b: (b, 0, 0))],
        out_shape=[jax.ShapeDtypeStruct((nb, MEM_LEN, D_MODEL), BF16)] * 2,
        compiler_params=pltpu.CompilerParams(dimension_semantics=("arbitrary",),
                                             vmem_limit_bytes=VMEM_LIMIT),
        name="memkv",
    )(mem, g, w)


INPROJ_TM = 256
INPROJ_CW = 512


def _inproj_kernel(x_ref, g_ref, w_ref, o_ref, h_ref):
    x = x_ref[...]
    ms = jnp.mean(x * x, axis=-1, keepdims=True)
    h_ref[...] = (x * lax.rsqrt(ms + RMS_EPS) * g_ref[...]).astype(BF16)
    for c in range(PROJ_WIDTH // INPROJ_CW):
        cols = slice(c * INPROJ_CW, (c + 1) * INPROJ_CW)
        o_ref[:, cols] = jnp.dot(h_ref[...], w_ref[:, cols],
                                 preferred_element_type=F32).astype(BF16)


def _inproj(x2, g, w):
    t = x2.shape[0]
    return pl.pallas_call(
        _inproj_kernel,
        grid=(t // INPROJ_TM,),
        in_specs=[pl.BlockSpec((INPROJ_TM, D_MODEL), lambda i: (i, 0)),
                  _resident((1, D_MODEL)),
                  _resident((D_MODEL, PROJ_WIDTH))],
        out_specs=pl.BlockSpec((INPROJ_TM, PROJ_WIDTH), lambda i: (i, 0)),
        out_shape=jax.ShapeDtypeStruct((t, PROJ_WIDTH), BF16),
        scratch_shapes=[pltpu.VMEM((INPROJ_TM, D_MODEL), BF16)],
        compiler_params=pltpu.CompilerParams(dimension_semantics=("arbitrary",),
                                             vmem_limit_bytes=VMEM_LIMIT),
        name="inproj",
    )(x2, g, w)


SWA_CH = 64


def _swa_kernel(sinks_ref, bucket_ref, rbt_ref, q_ref, z_ref, kvc_ref, kvp_ref, o_ref, bias_ref, s_ref, p_ref):
    b = pl.program_id(0)
    n = pl.program_id(1)

    @pl.when((b == 0) & (n == 0))
    def _():
        bucket = bucket_ref[...]
        rbt = rbt_ref[...]
        r0 = jnp.full((N_HEADS, 2 * BLOCK), NEG, F32)
        for bk in range(NUM_BUCKETS):
            r0 = jnp.where(bucket == bk, rbt[:, bk:bk + 1], r0)
        col = lax.broadcasted_iota(jnp.int32, (BLOCK, 2 * BLOCK), 1)
        for h in range(N_HEADS):
            t = jnp.broadcast_to(r0[h:h + 1, :], (BLOCK, 2 * BLOCK))
            t = pltpu.roll(t, 0, 1, stride=1, stride_axis=0)
            bias_ref[0, h * BLOCK:(h + 1) * BLOCK, :] = t
            bias_ref[1, h * BLOCK:(h + 1) * BLOCK, :] = jnp.where(col < BLOCK, NEG, t)

    first = jnp.where(n == 0, 1, 0)
    lo = lax.broadcasted_iota(jnp.int32, (BLOCK, 2 * HEAD_DIM), 1) < HEAD_DIM
    qscale = jnp.asarray(HEAD_DIM ** -0.5, BF16)
    zero = jnp.zeros((), BF16)
    rows_g = GROUP * BLOCK
    for g in range(N_KV_HEADS):
        kcols = slice(g * 2 * HEAD_DIM, (g + 1) * 2 * HEAD_DIM)
        vcols = slice((N_KV_HEADS + g) * 2 * HEAD_DIM, (N_KV_HEADS + g + 1) * 2 * HEAD_DIM)
        kk = jnp.concatenate([kvp_ref[:, kcols], kvc_ref[:, kcols]], axis=0)
        vv = jnp.concatenate([kvp_ref[:, vcols], kvc_ref[:, vcols]], axis=0)
        parts = []
        for j in range(GROUP // 2):
            pair = g * (GROUP // 2) + j
            qp = q_ref[:, pair * 2 * HEAD_DIM:(pair + 1) * 2 * HEAD_DIM] * qscale
            parts.append(jnp.where(lo, qp, zero))
            parts.append(jnp.where(lo, zero, qp))
        qs = jnp.concatenate(parts, axis=0)
        s_ref[...] = lax.dot_general(qs, kk, (((1,), (1,)), ((), ())), preferred_element_type=F32)
        for ch in range(rows_g // SWA_CH):
            rows = slice(ch * SWA_CH, (ch + 1) * SWA_CH)
            sink = sinks_ref[g * GROUP + (ch * SWA_CH) // BLOCK]
            s = s_ref[rows, :] + bias_ref[first, pl.ds(g * rows_g + ch * SWA_CH, SWA_CH), :]
            m = jnp.maximum(jnp.max(s, axis=-1, keepdims=True), sink)
            e = jnp.exp(s - m)
            denom = jnp.sum(e, axis=-1, keepdims=True) + jnp.exp(sink - m)
            p_ref[rows, :] = (e * (1.0 / denom)).astype(BF16)
        o = jnp.dot(p_ref[...], vv, preferred_element_type=F32)
        for j in range(GROUP // 2):
            pair = g * (GROUP // 2) + j
            lanes = slice(pair * 2 * HEAD_DIM, (pair + 1) * 2 * HEAD_DIM)
            oe = o[(2 * j) * BLOCK:(2 * j + 1) * BLOCK]
            oo = o[(2 * j + 1) * BLOCK:(2 * j + 2) * BLOCK]
            o_ref[:, lanes] = (jnp.where(lo, oe, oo) * _silu(z_ref[:, lanes].astype(F32))).astype(BF16)


def _swa(proj, sinks, bucket_row, rbt, nb, s):
    t = proj.shape[0]
    nblk = s // BLOCK
    row = lambda b, n: b * nblk + n
    return pl.pallas_call(
        _swa_kernel,
        grid=(nb, nblk),
        in_specs=[pl.BlockSpec(memory_space=pltpu.SMEM),
                  _resident((1, 2 * BLOCK)),
                  _resident((N_HEADS, NUM_BUCKETS)),
                  pl.BlockSpec((BLOCK, D_MODEL), lambda b, n: (row(b, n), COL_QA)),
                  pl.BlockSpec((BLOCK, D_MODEL), lambda b, n: (row(b, n), COL_ZA)),
                  pl.BlockSpec((BLOCK, KV_TAIL), lambda b, n: (row(b, n), KV_BLOCK_COL)),
                  pl.BlockSpec((BLOCK, KV_TAIL),
                               lambda b, n: (b * nblk + jnp.maximum(n - 1, 0), KV_BLOCK_COL))],
        out_specs=pl.BlockSpec((BLOCK, D_MODEL), lambda b, n: (row(b, n), 0)),
        out_shape=jax.ShapeDtypeStruct((t, D_MODEL), BF16),
        scratch_shapes=[pltpu.VMEM((2, N_HEADS * BLOCK, 2 * BLOCK), F32),
                        pltpu.VMEM((GROUP * BLOCK, 2 * BLOCK), F32),
                        pltpu.VMEM((GROUP * BLOCK, 2 * BLOCK), BF16)],
        compiler_params=pltpu.CompilerParams(dimension_semantics=("arbitrary", "arbitrary"),
                                             vmem_limit_bytes=VMEM_LIMIT),
        name="swa",
    )(sinks, bucket_row, rbt, proj, proj, proj, proj)


CONV_TC = 256
CONV_R = 64
LANES = 128
SUBLANES = 8


def _conv_kernel(ua_ref, ub_ref, zc_ref, cw_ref, cb_ref, lg_ref, lb_ref, o_ref, gext_ref, y_ref):
    n = pl.program_id(1)

    @pl.when(n == 0)
    def _():
        gext_ref[0:HALO, :] = jnp.zeros((HALO, D_MODEL), F32)

    @pl.when(n > 0)
    def _():
        gext_ref[0:HALO, :] = gext_ref[CONV_TC:CONV_TC + HALO, :]

    gext_ref[HALO:HALO + CONV_TC, :] = ua_ref[...].astype(F32) * jax.nn.sigmoid(ub_ref[...].astype(F32))

    off = HALO - (CONV_KERNEL - 1)
    win_rows = CONV_R + HALO
    for c in range(D_MODEL // LANES):
        lanes = slice(c * LANES, (c + 1) * LANES)
        for r in range(CONV_TC // CONV_R):
            win = gext_ref[r * CONV_R:r * CONV_R + win_rows, lanes]
            acc = jnp.zeros((CONV_R, LANES), F32)
            for res in range(SUBLANES):
                shifted = win if res == 0 else pltpu.roll(win, win_rows - res, 0)
                for k in range(CONV_KERNEL):
                    if (off + k) % SUBLANES == res:
                        a = off + k - res
                        acc = acc + cw_ref[k:k + 1, lanes] * shifted[a:a + CONV_R]
            y_ref[r * CONV_R:(r + 1) * CONV_R, lanes] = acc + cb_ref[:, lanes]

    y = y_ref[...]
    mu = jnp.mean(y, axis=-1, keepdims=True)
    yc = y - mu
    var = jnp.mean(yc * yc, axis=-1, keepdims=True)
    ln = yc * lax.rsqrt(var + LN_EPS) * lg_ref[...] + lb_ref[...]
    o_ref[...] = (_silu(ln) * _silu(zc_ref[...].astype(F32))).astype(BF16)


def _conv(proj, cw, cb, lg, lb, nb, s):
    t = proj.shape[0]
    nt = s // CONV_TC
    row = lambda b, n: b * nt + n
    return pl.pallas_call(
        _conv_kernel,
        grid=(nb, nt),
        in_specs=[pl.BlockSpec((CONV_TC, D_MODEL), lambda b, n: (row(b, n), COL_UA)),
                  pl.BlockSpec((CONV_TC, D_MODEL), lambda b, n: (row(b, n), COL_UB)),
                  pl.BlockSpec((CONV_TC, D_MODEL), lambda b, n: (row(b, n), COL_ZC)),
                  _resident((CONV_KERNEL, D_MODEL)),
                  _resident((1, D_MODEL)),
                  _resident((1, D_MODEL)),
                  _resident((1, D_MODEL))],
        out_specs=pl.BlockSpec((CONV_TC, D_MODEL), lambda b, n: (row(b, n), 0)),
        out_shape=jax.ShapeDtypeStruct((t, D_MODEL), BF16),
        scratch_shapes=[pltpu.VMEM((CONV_TC + HALO, D_MODEL), F32),
                        pltpu.VMEM((CONV_TC, D_MODEL), F32)],
        compiler_params=pltpu.CompilerParams(dimension_semantics=("arbitrary", "arbitrary"),
                                             vmem_limit_bytes=VMEM_LIMIT),
        name="conv",
    )(proj, proj, proj, cw, cb, lg, lb)


MERGE_TM = 256


def _merge_kernel(x_ref, qm_ref, zm_ref, g0_ref, g1_ref, g2_ref, oa_ref, oc_ref, mk_ref, mv_ref,
                  wb_ref, wo_ref, fg_ref, out_ref):
    qm = qm_ref[...] * jnp.asarray(MEM_HEAD_DIM ** -0.5, BF16)
    heads = []
    for h in range(MEM_HEADS):
        cols = slice(h * MEM_HEAD_DIM, (h + 1) * MEM_HEAD_DIM)
        s = lax.dot_general(qm[:, cols], mk_ref[0, :, cols], (((1,), (1,)), ((), ())),
                            preferred_element_type=F32)
        m = jnp.max(s, axis=-1, keepdims=True)
        p = jnp.exp(s - m)
        denom = jnp.sum(p, axis=-1, keepdims=True)
        o = jnp.dot(p.astype(BF16), mv_ref[0, :, cols], preferred_element_type=F32)
        heads.append(o * (1.0 / denom))
    om = (jnp.concatenate(heads, axis=1) * _silu(zm_ref[...].astype(F32))).astype(BF16)

    y = jax.nn.sigmoid(g0_ref[...].astype(F32)) * jnp.dot(oa_ref[...], wb_ref[0], preferred_element_type=F32)
    y = y + jax.nn.sigmoid(g1_ref[...].astype(F32)) * jnp.dot(oc_ref[...], wb_ref[1], preferred_element_type=F32)
    y = y + jax.nn.sigmoid(g2_ref[...].astype(F32)) * jnp.dot(om, wb_ref[2], preferred_element_type=F32)
    xo = x_ref[...] + jnp.dot(y.astype(BF16), wo_ref[...], preferred_element_type=F32)
    ms = jnp.mean(xo * xo, axis=-1, keepdims=True)
    out_ref[...] = xo * lax.rsqrt(ms + RMS_EPS) * fg_ref[...]


def _merge(x2, proj, oa, oc, mk, mv, wb, wo, fg, s):
    t = x2.shape[0]
    per_b = s // MERGE_TM
    col = lambda c: pl.BlockSpec((MERGE_TM, D_MODEL), lambda i: (i, c))
    memspec = pl.BlockSpec((1, MEM_LEN, D_MODEL), lambda i: (i // per_b, 0, 0))
    return pl.pallas_call(
        _merge_kernel,
        grid=(t // MERGE_TM,),
        in_specs=[col(0), col(COL_QM), col(COL_ZM), col(COL_G0), col(COL_G0 + 1), col(COL_G0 + 2),
                  col(0), col(0), memspec, memspec,
                  _resident((3, D_MODEL, D_MODEL)),
                  _resident((D_MODEL, D_MODEL)),
                  _resident((1, D_MODEL))],
        out_specs=pl.BlockSpec((MERGE_TM, D_MODEL), lambda i: (i, 0)),
        out_shape=jax.ShapeDtypeStruct((t, D_MODEL), F32),
        compiler_params=pltpu.CompilerParams(dimension_semantics=("arbitrary",),
                                             vmem_limit_bytes=VMEM_LIMIT),
        name="merge",
    )(x2, proj, proj, proj, proj, proj, oa, oc, mk, mv, wb, wo, fg)


def _bucket_row():
    j = jnp.arange(2 * BLOCK)
    d = BLOCK - j
    n = jnp.maximum(d, 0)
    max_exact = NUM_BUCKETS // 2
    nf = jnp.maximum(n, 1).astype(F32)
    large = max_exact + (jnp.log(nf / max_exact) / math.log(MAX_DISTANCE / max_exact)
                         * (NUM_BUCKETS - max_exact)).astype(jnp.int32)
    large = jnp.minimum(large, NUM_BUCKETS - 1)
    bucket = jnp.where(n < max_exact, n, large)
    valid = (d >= 0) & (d < WINDOW)
    return jnp.where(valid, bucket, -1).astype(jnp.int32).reshape(1, 2 * BLOCK)


def kernel(x, mem, rel_bias, norm_g, w_in, sinks, conv_w, conv_b, conv_ln_g, conv_ln_b, mem_norm_g,
           w_mem_kv, w_branch, w_out, final_norm_g):
    nb, s, d = x.shape
    assert d == D_MODEL and w_in.shape[0] == 1 and s % CONV_TC == 0 and s % MERGE_TM == 0
    x2 = x.reshape(nb * s, d)

    w = w_in[0]
    kv = [w[:, D_MODEL + i * HEAD_DIM:D_MODEL + (i + 1) * HEAD_DIM] for i in range(2 * N_KV_HEADS)]
    w_perm = jnp.concatenate(
        [w[:, :D_MODEL], w[:, D_MODEL + 2 * N_KV_HEADS * HEAD_DIM:]] + [kv[i // 2] for i in range(8)],
        axis=1).astype(BF16)

    mk, mv = _memkv(mem, mem_norm_g[0].reshape(1, d), w_mem_kv[0].astype(BF16))
    proj = _inproj(x2, norm_g[0].reshape(1, d), w_perm)
    oa = _swa(proj, sinks[0], _bucket_row(), rel_bias.T, nb, s)
    oc = _conv(proj, conv_w[0], conv_b[0].reshape(1, d), conv_ln_g[0].reshape(1, d),
               conv_ln_b[0].reshape(1, d), nb, s)
    out = _merge(x2, proj, oa, oc, mk, mv, w_branch[0].astype(BF16), w_out[0].astype(BF16),
                 final_norm_g.reshape(1, d), s)
    return out.reshape(nb, s, d)
```

```python
import functools
import math

import jax
import jax.numpy as jnp
from jax import lax
from jax.experimental import pallas as pl
from jax.experimental.pallas import tpu as pltpu

F32 = jnp.float32
BF16 = jnp.bfloat16

D_MODEL = 1024
N_HEADS = 16
N_KV_HEADS = 2
HEAD_DIM = 64
GROUP = N_HEADS // N_KV_HEADS
WINDOW = 128
BLOCK = 128
NUM_BUCKETS = 32
MAX_DISTANCE = 128
CONV_KERNEL = 31
MEM_LEN = 256
MEM_HEADS = 4
MEM_HEAD_DIM = 256
N_BRANCHES = 3
RMS_EPS = 1e-6
LN_EPS = 1e-5
NEG = -1e30

OFF_QA = 0
OFF_KV = OFF_QA + N_HEADS * HEAD_DIM
OFF_ZA = OFF_KV + 2 * N_KV_HEADS * HEAD_DIM
OFF_UA = OFF_ZA + D_MODEL
OFF_UB = OFF_UA + D_MODEL
OFF_ZC = OFF_UB + D_MODEL
OFF_QM = OFF_ZC + D_MODEL
OFF_ZM = OFF_QM + D_MODEL
OFF_G = OFF_ZM + D_MODEL
MIX_WIDTH = OFF_G

LANES = 128
SUBLANES = 8
HALO = 32
VMEM_LIMIT = 56 * 1024 * 1024

TM = 256
TILES_PER_STEP = 2
CW = 512
SWA_CH = 64
CONV_R = 64


def _resident(shape):
    return pl.BlockSpec(shape, lambda *_: (0,) * len(shape), pipeline_mode=pl.Buffered(1))


def _silu(z):
    return z * jax.nn.sigmoid(z)


def _rms(x, g):
    ms = jnp.mean(x * x, axis=-1, keepdims=True)
    return x * lax.rsqrt(ms + RMS_EPS) * g


def _memkv_kernel(mem_ref, g_ref, w_ref, mk_ref, mv_ref):
    mn = _rms(mem_ref[0], g_ref[...]).astype(BF16)
    kv = jnp.dot(mn, w_ref[...], preferred_element_type=F32)
    mk_ref[0] = kv[:, :D_MODEL].astype(BF16)
    mv_ref[0] = kv[:, D_MODEL:].astype(BF16)


def _memkv(mem, g, w):
    nb = mem.shape[0]
    return pl.pallas_call(
        _memkv_kernel,
        grid=(nb,),
        in_specs=[pl.BlockSpec((1, MEM_LEN, D_MODEL), lambda b: (b, 0, 0)),
                  _resident((1, D_MODEL)),
                  _resident((D_MODEL, 2 * D_MODEL))],
        out_specs=[pl.BlockSpec((1, MEM_LEN, D_MODEL), lambda b: (b, 0, 0)),
                   pl.BlockSpec((1, MEM_LEN, D_MODEL), lambda b: (b, 0, 0))],
        out_shape=[jax.ShapeDtypeStruct((nb, MEM_LEN, D_MODEL), BF16)] * 2,
        compiler_params=pltpu.CompilerParams(dimension_semantics=("arbitrary",),
                                             vmem_limit_bytes=VMEM_LIMIT),
        name="memkv",
    )(mem, g, w)


def _project(x_ref, ng_ref, w_ref, slot, halo, h_ref, qa_ref, kv_ref, za_ref, gext_ref, zc_ref, qm_ref, zm_ref):
    h_ref[slot] = _rms(x_ref[...], ng_ref[...]).astype(BF16)

    def proj(off, width):
        return jnp.dot(h_ref[slot], w_ref[:, off:off + width], preferred_element_type=F32)

    for c in range(D_MODEL // CW):
        cols = slice(c * CW, (c + 1) * CW)
        qa_ref[slot, :, cols] = (proj(OFF_QA + c * CW, CW) * (HEAD_DIM ** -0.5)).astype(BF16)

    kvf = proj(OFF_KV, 2 * N_KV_HEADS * HEAD_DIM)
    lo = lax.broadcasted_iota(jnp.int32, (TM, LANES), 1) < HEAD_DIM
    for t in range(2):
        xx = kvf[:, t * LANES:(t + 1) * LANES]
        rr = pltpu.roll(xx, HEAD_DIM, 1)
        kv_ref[slot, :, (2 * t) * LANES:(2 * t + 1) * LANES] = jnp.where(lo, xx, rr).astype(BF16)
        kv_ref[slot, :, (2 * t + 1) * LANES:(2 * t + 2) * LANES] = jnp.where(lo, rr, xx).astype(BF16)

    for c in range(D_MODEL // CW):
        cols = slice(c * CW, (c + 1) * CW)
        za_ref[slot, :, cols] = _silu(proj(OFF_ZA + c * CW, CW)).astype(BF16)

    gext_ref[slot, 0:HALO, :] = halo
    for c in range(D_MODEL // CW):
        cols = slice(c * CW, (c + 1) * CW)
        a = proj(OFF_UA + c * CW, CW)
        b = proj(OFF_UB + c * CW, CW)
        gext_ref[slot, HALO:HALO + TM, cols] = a * jax.nn.sigmoid(b)

    for c in range(D_MODEL // CW):
        cols = slice(c * CW, (c + 1) * CW)
        zc_ref[slot, :, cols] = _silu(proj(OFF_ZC + c * CW, CW)).astype(BF16)
    for c in range(D_MODEL // CW):
        cols = slice(c * CW, (c + 1) * CW)
        qm_ref[slot, :, cols] = (proj(OFF_QM + c * CW, CW) * (MEM_HEAD_DIM ** -0.5)).astype(BF16)
    for c in range(D_MODEL // CW):
        cols = slice(c * CW, (c + 1) * CW)
        zm_ref[slot, :, cols] = _silu(proj(OFF_ZM + c * CW, CW)).astype(BF16)


def _mix(slot, first, out_rows, sinks_ref, cw_ref, cb_ref, lg_ref, lb_ref, mk_ref, mv_ref,
         oa_ref, oc_ref, om_ref,
         qa_ref, kv_ref, za_ref, gext_ref, zc_ref, qm_ref, zm_ref,
         bias_ref, kvprev_ref, s_ref, p_ref, y_ref):
    lo = lax.broadcasted_iota(jnp.int32, (BLOCK, LANES), 1) < HEAD_DIM
    zero = jnp.zeros((), BF16)
    rows_g = GROUP * BLOCK

    for bi in range(TM // BLOCK):
        rows = slice(bi * BLOCK, (bi + 1) * BLOCK)
        variant = first if bi == 0 else 0
        for g in range(N_KV_HEADS):
            kcols = slice(g * LANES, (g + 1) * LANES)
            vcols = slice((N_KV_HEADS + g) * LANES, (N_KV_HEADS + g + 1) * LANES)
            if bi == 0:
                kprev, vprev = kvprev_ref[:, kcols], kvprev_ref[:, vcols]
            else:
                prev = slice((bi - 1) * BLOCK, bi * BLOCK)
                kprev, vprev = kv_ref[slot, prev, kcols], kv_ref[slot, prev, vcols]
            kk = jnp.concatenate([kprev, kv_ref[slot, rows, kcols]], axis=0)
            vv = jnp.concatenate([vprev, kv_ref[slot, rows, vcols]], axis=0)
            parts = []
            for j in range(GROUP // 2):
                pair = g * (GROUP // 2) + j
                qp = qa_ref[slot, rows, pair * LANES:(pair + 1) * LANES]
                parts.append(jnp.where(lo, qp, zero))
                parts.append(jnp.where(lo, zero, qp))
            qs = jnp.concatenate(parts, axis=0)
            s_ref[g] = lax.dot_general(qs, kk, (((1,), (1,)), ((), ())), preferred_element_type=F32)
            for ch in range(rows_g // SWA_CH):
                crow = slice(ch * SWA_CH, (ch + 1) * SWA_CH)
                sink = sinks_ref[g * GROUP + (ch * SWA_CH) // BLOCK]
                s = s_ref[g, crow, :] + bias_ref[variant, pl.ds(g * rows_g + ch * SWA_CH, SWA_CH), :]
                m = jnp.maximum(jnp.max(s, axis=-1, keepdims=True), sink)
                e = jnp.exp(s - m)
                denom = jnp.sum(e, axis=-1, keepdims=True) + jnp.exp(sink - m)
                p_ref[g, crow, :] = (e * (1.0 / denom)).astype(BF16)
            o = jnp.dot(p_ref[g], vv, preferred_element_type=F32)
            for j in range(GROUP // 2):
                pair = g * (GROUP // 2) + j
                lanes = slice(pair * LANES, (pair + 1) * LANES)
                oe = o[(2 * j) * BLOCK:(2 * j + 1) * BLOCK]
                oo = o[(2 * j + 1) * BLOCK:(2 * j + 2) * BLOCK]
                orow = slice(out_rows + bi * BLOCK, out_rows + (bi + 1) * BLOCK)
                oa_ref[orow, lanes] = (jnp.where(lo, oe, oo) * za_ref[slot, rows, lanes].astype(F32)).astype(BF16)
    kvprev_ref[...] = kv_ref[slot, TM - BLOCK:TM, :]

    off = HALO - (CONV_KERNEL - 1)
    win_rows = CONV_R + HALO
    for c in range(D_MODEL // LANES):
        lanes = slice(c * LANES, (c + 1) * LANES)
        for r in range(TM // CONV_R):
            win = gext_ref[slot, r * CONV_R:r * CONV_R + win_rows, lanes]
            acc = jnp.zeros((CONV_R, LANES), F32)
            for res in range(SUBLANES):
                shifted = win if res == 0 else pltpu.roll(win, win_rows - res, 0)
                for k in range(CONV_KERNEL):
                    if (off + k) % SUBLANES == res:
                        a = off + k - res
                        acc = acc + cw_ref[k:k + 1, lanes] * shifted[a:a + CONV_R]
            y_ref[r * CONV_R:(r + 1) * CONV_R, lanes] = acc + cb_ref[:, lanes]
    y = y_ref[...]
    mu = jnp.mean(y, axis=-1, keepdims=True)
    yc = y - mu
    var = jnp.mean(yc * yc, axis=-1, keepdims=True)
    ln = yc * lax.rsqrt(var + LN_EPS) * lg_ref[...] + lb_ref[...]
    oc_ref[out_rows:out_rows + TM, :] = (_silu(ln) * zc_ref[slot].astype(F32)).astype(BF16)

    for h in range(MEM_HEADS):
        cols = slice(h * MEM_HEAD_DIM, (h + 1) * MEM_HEAD_DIM)
        s = lax.dot_general(qm_ref[slot, :, cols], mk_ref[0, :, cols], (((1,), (1,)), ((), ())),
                            preferred_element_type=F32)
        m = jnp.max(s, axis=-1, keepdims=True)
        e = jnp.exp(s - m)
        p = (e * (1.0 / jnp.sum(e, axis=-1, keepdims=True))).astype(BF16)
        o = jnp.dot(p, mv_ref[0, :, cols], preferred_element_type=F32)
        om_ref[out_rows:out_rows + TM, cols] = (o * zm_ref[slot, :, cols].astype(F32)).astype(BF16)


def _mixers_kernel(sinks_ref, bucket_ref, rbt_ref, x0_ref, xa_ref, xb_ref, ng_ref, w_ref,
                   cw_ref, cb_ref, lg_ref, lb_ref, mk_ref, mv_ref,
                   oa_ref, oc_ref, om_ref,
                   h_ref, qa_ref, kv_ref, za_ref, gext_ref, zc_ref, qm_ref, zm_ref,
                   bias_ref, kvprev_ref, s_ref, p_ref, y_ref, *, tiles_per_seq):
    j = pl.program_id(0)
    stage = (h_ref, qa_ref, kv_ref, za_ref, gext_ref, zc_ref, qm_ref, zm_ref)
    mixargs = (sinks_ref, cw_ref, cb_ref, lg_ref, lb_ref, mk_ref, mv_ref, oa_ref, oc_ref, om_ref,
               qa_ref, kv_ref, za_ref, gext_ref, zc_ref, qm_ref, zm_ref,
               bias_ref, kvprev_ref, s_ref, p_ref, y_ref)
    zero_halo = jnp.zeros((HALO, D_MODEL), F32)

    @pl.when(j == 0)
    def _():
        bucket = bucket_ref[...]
        rbt = rbt_ref[...]
        r0 = jnp.full((N_HEADS, 2 * BLOCK), NEG, F32)
        for bk in range(NUM_BUCKETS):
            r0 = jnp.where(bucket == bk, rbt[:, bk:bk + 1], r0)
        col = lax.broadcasted_iota(jnp.int32, (BLOCK, 2 * BLOCK), 1)
        for h in range(N_HEADS):
            t = jnp.broadcast_to(r0[h:h + 1, :], (BLOCK, 2 * BLOCK))
            t = pltpu.roll(t, 0, 1, stride=1, stride_axis=0)
            bias_ref[0, h * BLOCK:(h + 1) * BLOCK, :] = t
            bias_ref[1, h * BLOCK:(h + 1) * BLOCK, :] = jnp.where(col < BLOCK, NEG, t)
        kvprev_ref[...] = jnp.zeros(kvprev_ref.shape, BF16)
        _project(x0_ref, ng_ref, w_ref, 0, zero_halo, *stage)

    steps_per_seq = tiles_per_seq // TILES_PER_STEP
    seq_start = (j % steps_per_seq) == 0
    next_seq_start = (j % steps_per_seq) == steps_per_seq - 1
    _mix(0, jnp.where(seq_start, 1, 0), 0, *mixargs)
    _project(xa_ref, ng_ref, w_ref, 1, gext_ref[0, TM:TM + HALO, :], *stage)
    _mix(1, 0, TM, *mixargs)
    halo = jnp.where(next_seq_start, zero_halo, gext_ref[1, TM:TM + HALO, :])
    _project(xb_ref, ng_ref, w_ref, 0, halo, *stage)


def _mixers(x2, sinks, bucket_row, rbt, ng, w_mix, cw, cb, lg, lb, mk, mv, s):
    t = x2.shape[0]
    n_tiles = t // TM
    tiles_per_seq = s // TM
    steps_per_seq = tiles_per_seq // TILES_PER_STEP
    xspec = lambda k: pl.BlockSpec(
        (TM, D_MODEL), lambda j: (jnp.minimum(TILES_PER_STEP * j + k, n_tiles - 1), 0))
    memspec = pl.BlockSpec((1, MEM_LEN, D_MODEL), lambda j: (j // steps_per_seq, 0, 0))
    ospec = pl.BlockSpec((TILES_PER_STEP * TM, D_MODEL), lambda j: (j, 0))
    return pl.pallas_call(
        functools.partial(_mixers_kernel, tiles_per_seq=tiles_per_seq),
        grid=(n_tiles // TILES_PER_STEP,),
        in_specs=[pl.BlockSpec(memory_space=pltpu.SMEM),
                  _resident((1, 2 * BLOCK)),
                  _resident((N_HEADS, NUM_BUCKETS)),
                  pl.BlockSpec((TM, D_MODEL), lambda j: (0, 0), pipeline_mode=pl.Buffered(1)),
                  xspec(1), xspec(2),
                  _resident((1, D_MODEL)),
                  _resident((D_MODEL, MIX_WIDTH)),
                  _resident((CONV_KERNEL, D_MODEL)),
                  _resident((1, D_MODEL)), _resident((1, D_MODEL)), _resident((1, D_MODEL)),
                  memspec, memspec],
        out_specs=[ospec, ospec, ospec],
        out_shape=[jax.ShapeDtypeStruct((t, D_MODEL), BF16)] * 3,
        scratch_shapes=[pltpu.VMEM((2, TM, D_MODEL), BF16),
                        pltpu.VMEM((2, TM, D_MODEL), BF16),
                        pltpu.VMEM((2, TM, 4 * LANES), BF16),
                        pltpu.VMEM((2, TM, D_MODEL), BF16),
                        pltpu.VMEM((2, TM + HALO, D_MODEL), F32),
                        pltpu.VMEM((2, TM, D_MODEL), BF16),
                        pltpu.VMEM((2, TM, D_MODEL), BF16),
                        pltpu.VMEM((2, TM, D_MODEL), BF16),
                        pltpu.VMEM((2, N_HEADS * BLOCK, 2 * BLOCK), F32),
                        pltpu.VMEM((BLOCK, 4 * LANES), BF16),
                        pltpu.VMEM((N_KV_HEADS, GROUP * BLOCK, 2 * BLOCK), F32),
                        pltpu.VMEM((N_KV_HEADS, GROUP * BLOCK, 2 * BLOCK), BF16),
                        pltpu.VMEM((TM, D_MODEL), F32)],
        compiler_params=pltpu.CompilerParams(dimension_semantics=("arbitrary",),
                                             vmem_limit_bytes=VMEM_LIMIT),
        name="mixers",
    )(sinks, bucket_row, rbt, x2, x2, x2, ng, w_mix, cw, cb, lg, lb, mk, mv)


MERGE_TM = 256


def _merge_kernel(x_ref, ng_ref, oa_ref, oc_ref, om_ref, wg_ref, wb_ref, wo_ref, fg_ref, out_ref):
    x = x_ref[...]
    h = _rms(x, ng_ref[...]).astype(BF16)
    y = None
    for n, o_ref in enumerate((oa_ref, oc_ref, om_ref)):
        gate = jax.nn.sigmoid(jnp.dot(h, wg_ref[:, n * D_MODEL:(n + 1) * D_MODEL], preferred_element_type=F32))
        term = gate * jnp.dot(o_ref[...], wb_ref[n], preferred_element_type=F32)
        y = term if y is None else y + term
    xo = x + jnp.dot(y.astype(BF16), wo_ref[...], preferred_element_type=F32)
    out_ref[...] = _rms(xo, fg_ref[...])


def _merge(x2, ng, oa, oc, om, wg, wb, wo, fg):
    t = x2.shape[0]
    tile = pl.BlockSpec((MERGE_TM, D_MODEL), lambda i: (i, 0))
    return pl.pallas_call(
        _merge_kernel,
        grid=(t // MERGE_TM,),
        in_specs=[tile, _resident((1, D_MODEL)), tile, tile, tile,
                  _resident((D_MODEL, N_BRANCHES * D_MODEL)),
                  _resident((N_BRANCHES, D_MODEL, D_MODEL)),
                  _resident((D_MODEL, D_MODEL)),
                  _resident((1, D_MODEL))],
        out_specs=tile,
        out_shape=jax.ShapeDtypeStruct((t, D_MODEL), F32),
        compiler_params=pltpu.CompilerParams(dimension_semantics=("arbitrary",),
                                             vmem_limit_bytes=VMEM_LIMIT),
        name="merge",
    )(x2, ng, oa, oc, om, wg, wb, wo, fg)


def _bucket_row():
    c = jnp.arange(2 * BLOCK)
    d = BLOCK - c
    n = jnp.maximum(d, 0)
    max_exact = NUM_BUCKETS // 2
    nf = jnp.maximum(n, 1).astype(F32)
    large = max_exact + (jnp.log(nf / max_exact) / math.log(MAX_DISTANCE / max_exact)
                         * (NUM_BUCKETS - max_exact)).astype(jnp.int32)
    large = jnp.minimum(large, NUM_BUCKETS - 1)
    bucket = jnp.where(n < max_exact, n, large)
    valid = (d >= 0) & (d < WINDOW)
    return jnp.where(valid, bucket, -1).astype(jnp.int32).reshape(1, 2 * BLOCK)


def kernel(x, mem, rel_bias, norm_g, w_in, sinks, conv_w, conv_b, conv_ln_g, conv_ln_b, mem_norm_g,
           w_mem_kv, w_branch, w_out, final_norm_g):
    nb, s, d = x.shape
    assert d == D_MODEL and w_in.shape[0] == 1 and s % (TILES_PER_STEP * TM) == 0
    x2 = x.reshape(nb * s, d)
    ng = norm_g[0].reshape(1, d)
    w_mix = w_in[0, :, :MIX_WIDTH].astype(BF16)
    w_gate = w_in[0, :, MIX_WIDTH:].astype(BF16)

    mk, mv = _memkv(mem, mem_norm_g[0].reshape(1, d), w_mem_kv[0].astype(BF16))
    oa, oc, om = _mixers(x2, sinks[0], _bucket_row(), rel_bias.T, ng, w_mix, conv_w[0],
                         conv_b[0].reshape(1, d), conv_ln_g[0].reshape(1, d), conv_ln_b[0].reshape(1, d),
                         mk, mv, s)
    out = _merge(x2, ng, oa, oc, om, w_gate, w_branch[0].astype(BF16), w_out[0].astype(BF16),
                 final_norm_g.reshape(1, d))
    return out.reshape(nb, s, d)
```

```python
import functools
import math

import jax
import jax.numpy as jnp
from jax import lax
from jax.experimental import pallas as pl
from jax.experimental.pallas import tpu as pltpu

F32 = jnp.float32
BF16 = jnp.bfloat16

D_MODEL = 1024
N_HEADS = 16
N_KV_HEADS = 2
HEAD_DIM = 64
GROUP = N_HEADS // N_KV_HEADS
WINDOW = 128
BLOCK = 128
NUM_BUCKETS = 32
MAX_DISTANCE = 128
CONV_KERNEL = 31
MEM_LEN = 256
MEM_HEADS = 4
MEM_HEAD_DIM = 256
N_BRANCHES = 3
RMS_EPS = 1e-6
LN_EPS = 1e-5
NEG = -1e30

OFF_QA = 0
OFF_KV = OFF_QA + N_HEADS * HEAD_DIM
OFF_ZA = OFF_KV + 2 * N_KV_HEADS * HEAD_DIM
OFF_UA = OFF_ZA + D_MODEL
OFF_UB = OFF_UA + D_MODEL
OFF_ZC = OFF_UB + D_MODEL
OFF_QM = OFF_ZC + D_MODEL
OFF_ZM = OFF_QM + D_MODEL
OFF_G = OFF_ZM + D_MODEL
MIX_WIDTH = OFF_G

LANES = 128
SUBLANES = 8
HALO = 32
VMEM_LIMIT = 56 * 1024 * 1024

TM = 256
TILES_PER_STEP = 2
CW = 512
SWA_CH = 32
CONV_R = 64


def _resident(shape):
    return pl.BlockSpec(shape, lambda *_: (0,) * len(shape), pipeline_mode=pl.Buffered(1))


def _silu(z):
    return z * jax.nn.sigmoid(z)


def _rms(x, g):
    ms = jnp.mean(x * x, axis=-1, keepdims=True)
    return x * lax.rsqrt(ms + RMS_EPS) * g


def _memkv_kernel(mem_ref, g_ref, w_ref, mk_ref, mv_ref):
    mn = _rms(mem_ref[0], g_ref[...]).astype(BF16)
    kv = jnp.dot(mn, w_ref[...], preferred_element_type=F32)
    mk_ref[0] = kv[:, :D_MODEL].astype(BF16)
    mv_ref[0] = kv[:, D_MODEL:].astype(BF16)


def _memkv(mem, g, w):
    nb = mem.shape[0]
    return pl.pallas_call(
        _memkv_kernel,
        grid=(nb,),
        in_specs=[pl.BlockSpec((1, MEM_LEN, D_MODEL), lambda b: (b, 0, 0)),
                  _resident((1, D_MODEL)),
                  _resident((D_MODEL, 2 * D_MODEL))],
        out_specs=[pl.BlockSpec((1, MEM_LEN, D_MODEL), lambda b: (b, 0, 0)),
                   pl.BlockSpec((1, MEM_LEN, D_MODEL), lambda b: (b, 0, 0))],
        out_shape=[jax.ShapeDtypeStruct((nb, MEM_LEN, D_MODEL), BF16)] * 2,
        compiler_params=pltpu.CompilerParams(dimension_semantics=("arbitrary",),
                                             vmem_limit_bytes=VMEM_LIMIT),
        name="memkv",
    )(mem, g, w)


def _interleave(items_a, items_b):
    total_a = sum(c for c, _ in items_a)
    total_b = sum(c for c, _ in items_b)
    ia = ib = 0
    done_a = done_b = 0.0
    while ia < len(items_a) or ib < len(items_b):
        take_a = ib >= len(items_b) or (ia < len(items_a) and done_a * total_b <= done_b * total_a)
        if take_a:
            cost, fn = items_a[ia]
            ia += 1
            done_a += cost
        else:
            cost, fn = items_b[ib]
            ib += 1
            done_b += cost
        fn()


def _project_items(x_ref, ng_ref, w_ref, slot, halo_fn,
                   h_ref, qa_ref, kv_ref, za_ref, gext_ref, zc_ref, qm_ref, zm_ref):
    items = []

    def proj(off, width):
        return jnp.dot(h_ref[slot], w_ref[:, off:off + width], preferred_element_type=F32)

    def norm():
        h_ref[slot] = _rms(x_ref[...], ng_ref[...]).astype(BF16)
    items.append((300, norm))

    def chunked(dst_ref, off, post):
        for c in range(D_MODEL // CW):
            def fn(c=c):
                dst_ref[slot, :, c * CW:(c + 1) * CW] = post(proj(off + c * CW, CW)).astype(BF16)
            items.append((1024, fn))

    chunked(qa_ref, OFF_QA, lambda v: v * (HEAD_DIM ** -0.5))

    def kv():
        kvf = proj(OFF_KV, 2 * N_KV_HEADS * HEAD_DIM)
        lo = lax.broadcasted_iota(jnp.int32, (TM, LANES), 1) < HEAD_DIM
        for t in range(2):
            xx = kvf[:, t * LANES:(t + 1) * LANES]
            rr = pltpu.roll(xx, HEAD_DIM, 1)
            kv_ref[slot, :, (2 * t) * LANES:(2 * t + 1) * LANES] = jnp.where(lo, xx, rr).astype(BF16)
            kv_ref[slot, :, (2 * t + 1) * LANES:(2 * t + 2) * LANES] = jnp.where(lo, rr, xx).astype(BF16)
    items.append((512, kv))

    chunked(za_ref, OFF_ZA, _silu)

    def halo():
        gext_ref[slot, 0:HALO, :] = halo_fn()
    items.append((10, halo))
    for c in range(D_MODEL // CW):
        def glu(c=c):
            a = proj(OFF_UA + c * CW, CW)
            b = proj(OFF_UB + c * CW, CW)
            gext_ref[slot, HALO:HALO + TM, c * CW:(c + 1) * CW] = a * jax.nn.sigmoid(b)
        items.append((2048, glu))

    chunked(zc_ref, OFF_ZC, _silu)
    chunked(qm_ref, OFF_QM, lambda v: v * (MEM_HEAD_DIM ** -0.5))
    chunked(zm_ref, OFF_ZM, _silu)
    return items


def _mix_items(slot, first, out_rows, sinks_ref, cwb_ref, cb_ref, lg_ref, lb_ref, mk_ref, mv_ref,
               oa_ref, oc_ref, om_ref,
               qa_ref, kv_ref, za_ref, gext_ref, zc_ref, qm_ref, zm_ref,
               bias_ref, kvprev_ref, s_ref, p_ref, y_ref):
    items = []
    rows_g = GROUP * BLOCK

    def lo_mask():
        return lax.broadcasted_iota(jnp.int32, (BLOCK, LANES), 1) < HEAD_DIM

    for bi in range(TM // BLOCK):
        rows = slice(bi * BLOCK, (bi + 1) * BLOCK)
        variant = first if bi == 0 else 0
        for g in range(N_KV_HEADS):
            kcols = slice(g * LANES, (g + 1) * LANES)
            vcols = slice((N_KV_HEADS + g) * LANES, (N_KV_HEADS + g + 1) * LANES)

            def band(cols, bi=bi, rows=rows):
                if bi == 0:
                    prev = kvprev_ref[:, cols]
                else:
                    prev = kv_ref[slot, (bi - 1) * BLOCK:bi * BLOCK, cols]
                return jnp.concatenate([prev, kv_ref[slot, rows, cols]], axis=0)

            def scores(g=g, rows=rows, kcols=kcols, band=band):
                lo = lo_mask()
                zero = jnp.zeros((), BF16)
                parts = []
                for j in range(GROUP // 2):
                    pair = g * (GROUP // 2) + j
                    qp = qa_ref[slot, rows, pair * LANES:(pair + 1) * LANES]
                    parts.append(jnp.where(lo, qp, zero))
                    parts.append(jnp.where(lo, zero, qp))
                qs = jnp.concatenate(parts, axis=0)
                s_ref[g] = lax.dot_general(qs, band(kcols), (((1,), (1,)), ((), ())),
                                           preferred_element_type=F32)
            items.append((600, scores))

            for ch in range(rows_g // SWA_CH):
                def softmax(g=g, ch=ch, variant=variant):
                    crow = slice(ch * SWA_CH, (ch + 1) * SWA_CH)
                    sink = sinks_ref[g * GROUP + (ch * SWA_CH) // BLOCK]
                    s = s_ref[g, crow, :] + bias_ref[variant, pl.ds(g * rows_g + ch * SWA_CH, SWA_CH), :]
                    m = jnp.maximum(jnp.max(s, axis=-1, keepdims=True), sink)
                    e = jnp.exp(s - m)
                    denom = jnp.sum(e, axis=-1, keepdims=True) + jnp.exp(sink - m)
                    p_ref[g, crow, :] = (e * (1.0 / denom)).astype(BF16)
                items.append((30, softmax))

            def values(g=g, bi=bi, rows=rows, vcols=vcols, band=band):
                lo = lo_mask()
                o = jnp.dot(p_ref[g], band(vcols), preferred_element_type=F32)
                orow = slice(out_rows + bi * BLOCK, out_rows + (bi + 1) * BLOCK)
                for j in range(GROUP // 2):
                    pair = g * (GROUP // 2) + j
                    lanes = slice(pair * LANES, (pair + 1) * LANES)
                    oe = o[(2 * j) * BLOCK:(2 * j + 1) * BLOCK]
                    oo = o[(2 * j + 1) * BLOCK:(2 * j + 2) * BLOCK]
                    oa_ref[orow, lanes] = (jnp.where(lo, oe, oo)
                                           * za_ref[slot, rows, lanes].astype(F32)).astype(BF16)
            items.append((700, values))

    def carry():
        kvprev_ref[...] = kv_ref[slot, TM - BLOCK:TM, :]
    items.append((10, carry))

    off = HALO - (CONV_KERNEL - 1)
    win_rows = CONV_R + HALO
    zrows = CONV_R + SUBLANES
    for c in range(D_MODEL // LANES):
        for r in range(TM // CONV_R):
            def conv(c=c, r=r):
                lanes = slice(c * LANES, (c + 1) * LANES)
                win = gext_ref[slot, r * CONV_R:r * CONV_R + win_rows, lanes]
                acc = None
                for res in range(SUBLANES):
                    nrow = CONV_R if res == 0 else zrows
                    z = None
                    for k in range(CONV_KERNEL):
                        if (off + k) % SUBLANES == res:
                            a = off + k - res
                            w = jnp.concatenate([cwb_ref[k, :, lanes]] * (nrow // SUBLANES), axis=0)
                            term = w * win[a:a + nrow]
                            z = term if z is None else z + term
                    if res:
                        z = pltpu.roll(z, zrows - res, 0)[:CONV_R]
                    acc = z if acc is None else acc + z
                y_ref[r * CONV_R:(r + 1) * CONV_R, lanes] = acc + cb_ref[:, lanes]
            items.append((200, conv))

    for r in range(TM // CONV_R):
        def layer_norm(r=r):
            rows = slice(r * CONV_R, (r + 1) * CONV_R)
            y = y_ref[rows, :]
            mu = jnp.mean(y, axis=-1, keepdims=True)
            yc = y - mu
            var = jnp.mean(yc * yc, axis=-1, keepdims=True)
            ln = yc * lax.rsqrt(var + LN_EPS) * lg_ref[...] + lb_ref[...]
            oc_ref[out_rows + r * CONV_R:out_rows + (r + 1) * CONV_R, :] = (
                _silu(ln) * zc_ref[slot, rows, :].astype(F32)).astype(BF16)
        items.append((300, layer_norm))

    for h in range(MEM_HEADS):
        def mem_head(h=h):
            cols = slice(h * MEM_HEAD_DIM, (h + 1) * MEM_HEAD_DIM)
            s = lax.dot_general(qm_ref[slot, :, cols], mk_ref[0, :, cols], (((1,), (1,)), ((), ())),
                                preferred_element_type=F32)
            m = jnp.max(s, axis=-1, keepdims=True)
            e = jnp.exp(s - m)
            p = (e * (1.0 / jnp.sum(e, axis=-1, keepdims=True))).astype(BF16)
            o = jnp.dot(p, mv_ref[0, :, cols], preferred_element_type=F32)
            om_ref[out_rows:out_rows + TM, cols] = (o * zm_ref[slot, :, cols].astype(F32)).astype(BF16)
        items.append((400, mem_head))
    return items


def _mixers_kernel(sinks_ref, bucket_ref, rbt_ref, x0_ref, xa_ref, xb_ref, ng_ref, w_ref,
                   cw_ref, cb_ref, lg_ref, lb_ref, mk_ref, mv_ref,
                   oa_ref, oc_ref, om_ref,
                   h_ref, qa_ref, kv_ref, za_ref, gext_ref, zc_ref, qm_ref, zm_ref,
                   bias_ref, kvprev_ref, s_ref, p_ref, y_ref, cwb_ref, *, tiles_per_seq):
    j = pl.program_id(0)
    stage = (h_ref, qa_ref, kv_ref, za_ref, gext_ref, zc_ref, qm_ref, zm_ref)
    mixargs = (sinks_ref, cwb_ref, cb_ref, lg_ref, lb_ref, mk_ref, mv_ref, oa_ref, oc_ref, om_ref,
               qa_ref, kv_ref, za_ref, gext_ref, zc_ref, qm_ref, zm_ref,
               bias_ref, kvprev_ref, s_ref, p_ref, y_ref)

    def zero_halo():
        return jnp.zeros((HALO, D_MODEL), F32)

    @pl.when(j == 0)
    def _():
        bucket = bucket_ref[...]
        rbt = rbt_ref[...]
        r0 = jnp.full((N_HEADS, 2 * BLOCK), NEG, F32)
        for bk in range(NUM_BUCKETS):
            r0 = jnp.where(bucket == bk, rbt[:, bk:bk + 1], r0)
        col = lax.broadcasted_iota(jnp.int32, (BLOCK, 2 * BLOCK), 1)
        for h in range(N_HEADS):
            t = jnp.broadcast_to(r0[h:h + 1, :], (BLOCK, 2 * BLOCK))
            t = pltpu.roll(t, 0, 1, stride=1, stride_axis=0)
            bias_ref[0, h * BLOCK:(h + 1) * BLOCK, :] = t
            bias_ref[1, h * BLOCK:(h + 1) * BLOCK, :] = jnp.where(col < BLOCK, NEG, t)
        kvprev_ref[...] = jnp.zeros(kvprev_ref.shape, BF16)
        for k in range(CONV_KERNEL):
            cwb_ref[k] = jnp.broadcast_to(cw_ref[k:k + 1, :], (SUBLANES, D_MODEL))
        for _, fn in _project_items(x0_ref, ng_ref, w_ref, 0, zero_halo, *stage):
            fn()

    steps_per_seq = tiles_per_seq // TILES_PER_STEP
    seq_start = (j % steps_per_seq) == 0
    next_seq_start = (j % steps_per_seq) == steps_per_seq - 1

    def halo_from(slot, zero_if=None):
        def fn():
            tail = gext_ref[slot, TM:TM + HALO, :]
            return tail if zero_if is None else jnp.where(zero_if, zero_halo(), tail)
        return fn

    _interleave(_mix_items(0, jnp.where(seq_start, 1, 0), 0, *mixargs),
                _project_items(xa_ref, ng_ref, w_ref, 1, halo_from(0), *stage))
    _interleave(_mix_items(1, 0, TM, *mixargs),
                _project_items(xb_ref, ng_ref, w_ref, 0, halo_from(1, next_seq_start), *stage))


def _mixers(x2, sinks, bucket_row, rbt, ng, w_mix, cw, cb, lg, lb, mk, mv, s):
    t = x2.shape[0]
    n_tiles = t // TM
    tiles_per_seq = s // TM
    steps_per_seq = tiles_per_seq // TILES_PER_STEP
    xspec = lambda k: pl.BlockSpec(
        (TM, D_MODEL), lambda j: (jnp.minimum(TILES_PER_STEP * j + k, n_tiles - 1), 0))
    memspec = pl.BlockSpec((1, MEM_LEN, D_MODEL), lambda j: (j // steps_per_seq, 0, 0))
    ospec = pl.BlockSpec((TILES_PER_STEP * TM, D_MODEL), lambda j: (j, 0))
    return pl.pallas_call(
        functools.partial(_mixers_kernel, tiles_per_seq=tiles_per_seq),
        grid=(n_tiles // TILES_PER_STEP,),
        in_specs=[pl.BlockSpec(memory_space=pltpu.SMEM),
                  _resident((1, 2 * BLOCK)),
                  _resident((N_HEADS, NUM_BUCKETS)),
                  pl.BlockSpec((TM, D_MODEL), lambda j: (0, 0), pipeline_mode=pl.Buffered(1)),
                  xspec(1), xspec(2),
                  _resident((1, D_MODEL)),
                  _resident((D_MODEL, MIX_WIDTH)),
                  _resident((CONV_KERNEL, D_MODEL)),
                  _resident((1, D_MODEL)), _resident((1, D_MODEL)), _resident((1, D_MODEL)),
                  memspec, memspec],
        out_specs=[ospec, ospec, ospec],
        out_shape=[jax.ShapeDtypeStruct((t, D_MODEL), BF16)] * 3,
        scratch_shapes=[pltpu.VMEM((2, TM, D_MODEL), BF16),
                        pltpu.VMEM((2, TM, D_MODEL), BF16),
                        pltpu.VMEM((2, TM, 4 * LANES), BF16),
                        pltpu.VMEM((2, TM, D_MODEL), BF16),
                        pltpu.VMEM((2, TM + HALO, D_MODEL), F32),
                        pltpu.VMEM((2, TM, D_MODEL), BF16),
                        pltpu.VMEM((2, TM, D_MODEL), BF16),
                        pltpu.VMEM((2, TM, D_MODEL), BF16),
                        pltpu.VMEM((2, N_HEADS * BLOCK, 2 * BLOCK), F32),
                        pltpu.VMEM((BLOCK, 4 * LANES), BF16),
                        pltpu.VMEM((N_KV_HEADS, GROUP * BLOCK, 2 * BLOCK), F32),
                        pltpu.VMEM((N_KV_HEADS, GROUP * BLOCK, 2 * BLOCK), BF16),
                        pltpu.VMEM((TM, D_MODEL), F32),
                        pltpu.VMEM((CONV_KERNEL, SUBLANES, D_MODEL), F32)],
        compiler_params=pltpu.CompilerParams(dimension_semantics=("arbitrary",),
                                             vmem_limit_bytes=VMEM_LIMIT),
        name="mixers",
    )(sinks, bucket_row, rbt, x2, x2, x2, ng, w_mix, cw, cb, lg, lb, mk, mv)


MERGE_TM = 256


def _merge_kernel(x_ref, ng_ref, oa_ref, oc_ref, om_ref, wg_ref, wb_ref, wo_ref, fg_ref, out_ref):
    x = x_ref[...]
    h = _rms(x, ng_ref[...]).astype(BF16)
    y = None
    for n, o_ref in enumerate((oa_ref, oc_ref, om_ref)):
        gate = jax.nn.sigmoid(jnp.dot(h, wg_ref[:, n * D_MODEL:(n + 1) * D_MODEL], preferred_element_type=F32))
        term = gate * jnp.dot(o_ref[...], wb_ref[n], preferred_element_type=F32)
        y = term if y is None else y + term
    xo = x + jnp.dot(y.astype(BF16), wo_ref[...], preferred_element_type=F32)
    out_ref[...] = _rms(xo, fg_ref[...])


def _merge(x2, ng, oa, oc, om, wg, wb, wo, fg):
    t = x2.shape[0]
    tile = pl.BlockSpec((MERGE_TM, D_MODEL), lambda i: (i, 0))
    return pl.pallas_call(
        _merge_kernel,
        grid=(t // MERGE_TM,),
        in_specs=[tile, _resident((1, D_MODEL)), tile, tile, tile,
                  _resident((D_MODEL, N_BRANCHES * D_MODEL)),
                  _resident((N_BRANCHES, D_MODEL, D_MODEL)),
                  _resident((D_MODEL, D_MODEL)),
                  _resident((1, D_MODEL))],
        out_specs=tile,
        out_shape=jax.ShapeDtypeStruct((t, D_MODEL), F32),
        compiler_params=pltpu.CompilerParams(dimension_semantics=("arbitrary",),
                                             vmem_limit_bytes=VMEM_LIMIT),
        name="merge",
    )(x2, ng, oa, oc, om, wg, wb, wo, fg)


def _bucket_row():
    c = jnp.arange(2 * BLOCK)
    d = BLOCK - c
    n = jnp.maximum(d, 0)
    max_exact = NUM_BUCKETS // 2
    nf = jnp.maximum(n, 1).astype(F32)
    large = max_exact + (jnp.log(nf / max_exact) / math.log(MAX_DISTANCE / max_exact)
                         * (NUM_BUCKETS - max_exact)).astype(jnp.int32)
    large = jnp.minimum(large, NUM_BUCKETS - 1)
    bucket = jnp.where(n < max_exact, n, large)
    valid = (d >= 0) & (d < WINDOW)
    return jnp.where(valid, bucket, -1).astype(jnp.int32).reshape(1, 2 * BLOCK)


def kernel(x, mem, rel_bias, norm_g, w_in, sinks, conv_w, conv_b, conv_ln_g, conv_ln_b, mem_norm_g,
           w_mem_kv, w_branch, w_out, final_norm_g):
    nb, s, d = x.shape
    assert d == D_MODEL and w_in.shape[0] == 1 and s % (TILES_PER_STEP * TM) == 0
    x2 = x.reshape(nb * s, d)
    ng = norm_g[0].reshape(1, d)
    w_mix = w_in[0, :, :MIX_WIDTH].astype(BF16)
    w_gate = w_in[0, :, MIX_WIDTH:].astype(BF16)

    mk, mv = _memkv(mem, mem_norm_g[0].reshape(1, d), w_mem_kv[0].astype(BF16))
    oa, oc, om = _mixers(x2, sinks[0], _bucket_row(), rel_bias.T, ng, w_mix, conv_w[0],
                         conv_b[0].reshape(1, d), conv_ln_g[0].reshape(1, d), conv_ln_b[0].reshape(1, d),
                         mk, mv, s)
    out = _merge(x2, ng, oa, oc, om, w_gate, w_branch[0].astype(BF16), w_out[0].astype(BF16),
                 final_norm_g.reshape(1, d))
    return out.reshape(nb, s, d)
```

```python
import functools
import math

import jax
import jax.numpy as jnp
from jax import lax
from jax.experimental import pallas as pl
from jax.experimental.pallas import tpu as pltpu

F32 = jnp.float32
BF16 = jnp.bfloat16

D_MODEL = 1024
N_HEADS = 16
N_KV_HEADS = 2
HEAD_DIM = 64
GROUP = N_HEADS // N_KV_HEADS
WINDOW = 128
BLOCK = 128
NUM_BUCKETS = 32
MAX_DISTANCE = 128
CONV_KERNEL = 31
MEM_LEN = 256
MEM_HEADS = 4
MEM_HEAD_DIM = 256
N_BRANCHES = 3
RMS_EPS = 1e-6
LN_EPS = 1e-5
NEG = -1e30

OFF_QA = 0
OFF_KV = OFF_QA + N_HEADS * HEAD_DIM
OFF_ZA = OFF_KV + 2 * N_KV_HEADS * HEAD_DIM
OFF_UA = OFF_ZA + D_MODEL
OFF_UB = OFF_UA + D_MODEL
OFF_ZC = OFF_UB + D_MODEL
OFF_QM = OFF_ZC + D_MODEL
OFF_ZM = OFF_QM + D_MODEL
OFF_G = OFF_ZM + D_MODEL
MIX_WIDTH = OFF_G

LANES = 128
SUBLANES = 8
HALO = 32
VMEM_LIMIT = 56 * 1024 * 1024

TM = 256
TILES_PER_STEP = 2
CW = 512
SWA_CH = 64
CONV_R = 64


def _resident(shape):
    return pl.BlockSpec(shape, lambda *_: (0,) * len(shape), pipeline_mode=pl.Buffered(1))


def _silu(z):
    return z * jax.nn.sigmoid(z)


def _rms(x, g):
    ms = jnp.mean(x * x, axis=-1, keepdims=True)
    return x * lax.rsqrt(ms + RMS_EPS) * g


def _memkv_kernel(mem_ref, g_ref, w_ref, mk_ref, mv_ref):
    mn = _rms(mem_ref[0], g_ref[...]).astype(BF16)
    kv = jnp.dot(mn, w_ref[...], preferred_element_type=F32)
    mk_ref[0] = kv[:, :D_MODEL].astype(BF16)
    mv_ref[0] = kv[:, D_MODEL:].astype(BF16)


def _memkv(mem, g, w):
    nb = mem.shape[0]
    return pl.pallas_call(
        _memkv_kernel,
        grid=(nb,),
        in_specs=[pl.BlockSpec((1, MEM_LEN, D_MODEL), lambda b: (b, 0, 0)),
                  _resident((1, D_MODEL)),
                  _resident((D_MODEL, 2 * D_MODEL))],
        out_specs=[pl.BlockSpec((1, MEM_LEN, D_MODEL), lambda b: (b, 0, 0)),
                   pl.BlockSpec((1, MEM_LEN, D_MODEL), lambda b: (b, 0, 0))],
        out_shape=[jax.ShapeDtypeStruct((nb, MEM_LEN, D_MODEL), BF16)] * 2,
        compiler_params=pltpu.CompilerParams(dimension_semantics=("arbitrary",),
                                             vmem_limit_bytes=VMEM_LIMIT),
        name="memkv",
    )(mem, g, w)


def _interleave(*lists):
    totals = [sum(c for c, _ in items) for items in lists]
    pos = [0] * len(lists)
    done = [0.0] * len(lists)
    while any(p < len(items) for p, items in zip(pos, lists)):
        live = [i for i, items in enumerate(lists) if pos[i] < len(items)]
        i = min(live, key=lambda i: done[i] / totals[i])
        cost, fn = lists[i][pos[i]]
        pos[i] += 1
        done[i] += cost
        fn()


def _project_items(x_ref, ng_ref, w_ref, slot, halo_fn,
                   h_ref, qa_ref, kv_ref, za_ref, gext_ref, zc_ref, qm_ref, zm_ref):
    items = []

    def proj(off, width):
        return jnp.dot(h_ref[slot], w_ref[:, off:off + width], preferred_element_type=F32)

    def norm():
        h_ref[slot] = _rms(x_ref[...], ng_ref[...]).astype(BF16)
    items.append((300, norm))

    def chunked(dst_ref, off, post):
        for c in range(D_MODEL // CW):
            def fn(c=c):
                dst_ref[slot, :, c * CW:(c + 1) * CW] = post(proj(off + c * CW, CW)).astype(BF16)
            items.append((1024, fn))

    chunked(qa_ref, OFF_QA, lambda v: v * (HEAD_DIM ** -0.5))

    def kv():
        kvf = proj(OFF_KV, 2 * N_KV_HEADS * HEAD_DIM)
        lo = lax.broadcasted_iota(jnp.int32, (TM, LANES), 1) < HEAD_DIM
        for t in range(2):
            xx = kvf[:, t * LANES:(t + 1) * LANES]
            rr = pltpu.roll(xx, HEAD_DIM, 1)
            kv_ref[slot, :, (2 * t) * LANES:(2 * t + 1) * LANES] = jnp.where(lo, xx, rr).astype(BF16)
            kv_ref[slot, :, (2 * t + 1) * LANES:(2 * t + 2) * LANES] = jnp.where(lo, rr, xx).astype(BF16)
    items.append((512, kv))

    chunked(za_ref, OFF_ZA, _silu)

    def halo():
        gext_ref[slot, 0:HALO, :] = halo_fn()
    items.append((10, halo))
    for c in range(D_MODEL // CW):
        def glu(c=c):
            a = proj(OFF_UA + c * CW, CW)
            b = proj(OFF_UB + c * CW, CW)
            gext_ref[slot, HALO:HALO + TM, c * CW:(c + 1) * CW] = a * jax.nn.sigmoid(b)
        items.append((2048, glu))

    chunked(zc_ref, OFF_ZC, _silu)
    chunked(qm_ref, OFF_QM, lambda v: v * (MEM_HEAD_DIM ** -0.5))
    chunked(zm_ref, OFF_ZM, _silu)
    return items


def _mix_items(slot, first, out_rows, sinks_ref, cw_ref, cb_ref, lg_ref, lb_ref, mk_ref, mv_ref,
               oa_ref, oc_ref, om_ref,
               qa_ref, kv_ref, za_ref, gext_ref, zc_ref, qm_ref, zm_ref,
               bias_ref, kvprev_ref, s_ref, p_ref, y_ref):
    items = []
    conv_items = []
    rows_g = GROUP * BLOCK

    def lo_mask():
        return lax.broadcasted_iota(jnp.int32, (BLOCK, LANES), 1) < HEAD_DIM

    for bi in range(TM // BLOCK):
        rows = slice(bi * BLOCK, (bi + 1) * BLOCK)
        variant = first if bi == 0 else 0
        for g in range(N_KV_HEADS):
            kcols = slice(g * LANES, (g + 1) * LANES)
            vcols = slice((N_KV_HEADS + g) * LANES, (N_KV_HEADS + g + 1) * LANES)

            def band(cols, bi=bi, rows=rows):
                if bi == 0:
                    prev = kvprev_ref[:, cols]
                else:
                    prev = kv_ref[slot, (bi - 1) * BLOCK:bi * BLOCK, cols]
                return jnp.concatenate([prev, kv_ref[slot, rows, cols]], axis=0)

            def scores(g=g, rows=rows, kcols=kcols, band=band):
                lo = lo_mask()
                zero = jnp.zeros((), BF16)
                parts = []
                for j in range(GROUP // 2):
                    pair = g * (GROUP // 2) + j
                    qp = qa_ref[slot, rows, pair * LANES:(pair + 1) * LANES]
                    parts.append(jnp.where(lo, qp, zero))
                    parts.append(jnp.where(lo, zero, qp))
                qs = jnp.concatenate(parts, axis=0)
                s_ref[g] = lax.dot_general(qs, band(kcols), (((1,), (1,)), ((), ())),
                                           preferred_element_type=F32)
            items.append((300, scores))

            for ch in range(rows_g // SWA_CH):
                def softmax(g=g, ch=ch, variant=variant):
                    crow = slice(ch * SWA_CH, (ch + 1) * SWA_CH)
                    sink = sinks_ref[g * GROUP + (ch * SWA_CH) // BLOCK]
                    s = s_ref[g, crow, :] + bias_ref[variant, pl.ds(g * rows_g + ch * SWA_CH, SWA_CH), :]
                    m = jnp.maximum(jnp.max(s, axis=-1, keepdims=True), sink)
                    e = jnp.exp(s - m)
                    denom = jnp.sum(e, axis=-1, keepdims=True) + jnp.exp(sink - m)
                    p_ref[g, crow, :] = (e * (1.0 / denom)).astype(BF16)
                items.append((40, softmax))

            def values(g=g, bi=bi, rows=rows, vcols=vcols, band=band):
                lo = lo_mask()
                o = jnp.dot(p_ref[g], band(vcols), preferred_element_type=F32)
                orow = slice(out_rows + bi * BLOCK, out_rows + (bi + 1) * BLOCK)
                for j in range(GROUP // 2):
                    pair = g * (GROUP // 2) + j
                    lanes = slice(pair * LANES, (pair + 1) * LANES)
                    oe = o[(2 * j) * BLOCK:(2 * j + 1) * BLOCK]
                    oo = o[(2 * j + 1) * BLOCK:(2 * j + 2) * BLOCK]
                    oa_ref[orow, lanes] = (jnp.where(lo, oe, oo)
                                           * za_ref[slot, rows, lanes].astype(F32)).astype(BF16)
            items.append((350, values))

    def carry():
        kvprev_ref[...] = kv_ref[slot, TM - BLOCK:TM, :]
    items.append((10, carry))

    off = HALO - (CONV_KERNEL - 1)
    win_rows = CONV_R + HALO
    for c in range(D_MODEL // LANES):
        for r in range(TM // CONV_R):
            def conv(c=c, r=r):
                lanes = slice(c * LANES, (c + 1) * LANES)
                win = gext_ref[slot, r * CONV_R:r * CONV_R + win_rows, lanes]
                acc = jnp.zeros((CONV_R, LANES), F32)
                for res in range(SUBLANES):
                    shifted = win if res == 0 else pltpu.roll(win, win_rows - res, 0)
                    for k in range(CONV_KERNEL):
                        if (off + k) % SUBLANES == res:
                            a = off + k - res
                            acc = acc + cw_ref[k:k + 1, lanes] * shifted[a:a + CONV_R]
                y_ref[r * CONV_R:(r + 1) * CONV_R, lanes] = acc + cb_ref[:, lanes]
            conv_items.append((200, conv))

    for r in range(TM // CONV_R):
        def layer_norm(r=r):
            rows = slice(r * CONV_R, (r + 1) * CONV_R)
            y = y_ref[rows, :]
            mu = jnp.mean(y, axis=-1, keepdims=True)
            yc = y - mu
            var = jnp.mean(yc * yc, axis=-1, keepdims=True)
            ln = yc * lax.rsqrt(var + LN_EPS) * lg_ref[...] + lb_ref[...]
            oc_ref[out_rows + r * CONV_R:out_rows + (r + 1) * CONV_R, :] = (
                _silu(ln) * zc_ref[slot, rows, :].astype(F32)).astype(BF16)
        conv_items.append((400, layer_norm))

    for h in range(MEM_HEADS):
        def mem_head(h=h):
            cols = slice(h * MEM_HEAD_DIM, (h + 1) * MEM_HEAD_DIM)
            s = lax.dot_general(qm_ref[slot, :, cols], mk_ref[0, :, cols], (((1,), (1,)), ((), ())),
                                preferred_element_type=F32)
            m = jnp.max(s, axis=-1, keepdims=True)
            e = jnp.exp(s - m)
            p = (e * (1.0 / jnp.sum(e, axis=-1, keepdims=True))).astype(BF16)
            o = jnp.dot(p, mv_ref[0, :, cols], preferred_element_type=F32)
            om_ref[out_rows:out_rows + TM, cols] = (o * zm_ref[slot, :, cols].astype(F32)).astype(BF16)
        items.append((450, mem_head))
    return items, conv_items


def _mixers_kernel(sinks_ref, bucket_ref, rbt_ref, x0_ref, xa_ref, xb_ref, ng_ref, w_ref,
                   cw_ref, cb_ref, lg_ref, lb_ref, mk_ref, mv_ref,
                   oa_ref, oc_ref, om_ref,
                   h_ref, qa_ref, kv_ref, za_ref, gext_ref, zc_ref, qm_ref, zm_ref,
                   bias_ref, kvprev_ref, s_ref, p_ref, y_ref, *, tiles_per_seq):
    j = pl.program_id(0)
    stage = (h_ref, qa_ref, kv_ref, za_ref, gext_ref, zc_ref, qm_ref, zm_ref)
    mixargs = (sinks_ref, cw_ref, cb_ref, lg_ref, lb_ref, mk_ref, mv_ref, oa_ref, oc_ref, om_ref,
               qa_ref, kv_ref, za_ref, gext_ref, zc_ref, qm_ref, zm_ref,
               bias_ref, kvprev_ref, s_ref, p_ref, y_ref)

    def zero_halo():
        return jnp.zeros((HALO, D_MODEL), F32)

    @pl.when(j == 0)
    def _():
        bucket = bucket_ref[...]
        rbt = rbt_ref[...]
        r0 = jnp.full((N_HEADS, 2 * BLOCK), NEG, F32)
        for bk in range(NUM_BUCKETS):
            r0 = jnp.where(bucket == bk, rbt[:, bk:bk + 1], r0)
        col = lax.broadcasted_iota(jnp.int32, (BLOCK, 2 * BLOCK), 1)
        for h in range(N_HEADS):
            t = jnp.broadcast_to(r0[h:h + 1, :], (BLOCK, 2 * BLOCK))
            t = pltpu.roll(t, 0, 1, stride=1, stride_axis=0)
            bias_ref[0, h * BLOCK:(h + 1) * BLOCK, :] = t
            bias_ref[1, h * BLOCK:(h + 1) * BLOCK, :] = jnp.where(col < BLOCK, NEG, t)
        kvprev_ref[...] = jnp.zeros(kvprev_ref.shape, BF16)
        for _, fn in _project_items(x0_ref, ng_ref, w_ref, 0, zero_halo, *stage):
            fn()

    steps_per_seq = tiles_per_seq // TILES_PER_STEP
    seq_start = (j % steps_per_seq) == 0
    next_seq_start = (j % steps_per_seq) == steps_per_seq - 1

    def halo_from(slot, zero_if=None):
        def fn():
            tail = gext_ref[slot, TM:TM + HALO, :]
            return tail if zero_if is None else jnp.where(zero_if, zero_halo(), tail)
        return fn

    _interleave(*_mix_items(0, jnp.where(seq_start, 1, 0), 0, *mixargs),
                _project_items(xa_ref, ng_ref, w_ref, 1, halo_from(0), *stage))
    _interleave(*_mix_items(1, 0, TM, *mixargs),
                _project_items(xb_ref, ng_ref, w_ref, 0, halo_from(1, next_seq_start), *stage))


def _mixers(x2, sinks, bucket_row, rbt, ng, w_mix, cw, cb, lg, lb, mk, mv, s):
    t = x2.shape[0]
    n_tiles = t // TM
    tiles_per_seq = s // TM
    steps_per_seq = tiles_per_seq // TILES_PER_STEP
    xspec = lambda k: pl.BlockSpec(
        (TM, D_MODEL), lambda j: (jnp.minimum(TILES_PER_STEP * j + k, n_tiles - 1), 0))
    memspec = pl.BlockSpec((1, MEM_LEN, D_MODEL), lambda j: (j // steps_per_seq, 0, 0))
    ospec = pl.BlockSpec((TILES_PER_STEP * TM, D_MODEL), lambda j: (j, 0))
    return pl.pallas_call(
        functools.partial(_mixers_kernel, tiles_per_seq=tiles_per_seq),
        grid=(n_tiles // TILES_PER_STEP,),
        in_specs=[pl.BlockSpec(memory_space=pltpu.SMEM),
                  _resident((1, 2 * BLOCK)),
                  _resident((N_HEADS, NUM_BUCKETS)),
                  pl.BlockSpec((TM, D_MODEL), lambda j: (0, 0), pipeline_mode=pl.Buffered(1)),
                  xspec(1), xspec(2),
                  _resident((1, D_MODEL)),
                  _resident((D_MODEL, MIX_WIDTH)),
                  _resident((CONV_KERNEL, D_MODEL)),
                  _resident((1, D_MODEL)), _resident((1, D_MODEL)), _resident((1, D_MODEL)),
                  memspec, memspec],
        out_specs=[ospec, ospec, ospec],
        out_shape=[jax.ShapeDtypeStruct((t, D_MODEL), BF16)] * 3,
        scratch_shapes=[pltpu.VMEM((2, TM, D_MODEL), BF16),
                        pltpu.VMEM((2, TM, D_MODEL), BF16),
                        pltpu.VMEM((2, TM, 4 * LANES), BF16),
                        pltpu.VMEM((2, TM, D_MODEL), BF16),
                        pltpu.VMEM((2, TM + HALO, D_MODEL), F32),
                        pltpu.VMEM((2, TM, D_MODEL), BF16),
                        pltpu.VMEM((2, TM, D_MODEL), BF16),
                        pltpu.VMEM((2, TM, D_MODEL), BF16),
                        pltpu.VMEM((2, N_HEADS * BLOCK, 2 * BLOCK), F32),
                        pltpu.VMEM((BLOCK, 4 * LANES), BF16),
                        pltpu.VMEM((N_KV_HEADS, GROUP * BLOCK, 2 * BLOCK), F32),
                        pltpu.VMEM((N_KV_HEADS, GROUP * BLOCK, 2 * BLOCK), BF16),
                        pltpu.VMEM((TM, D_MODEL), F32)],
        compiler_params=pltpu.CompilerParams(dimension_semantics=("arbitrary",),
                                             vmem_limit_bytes=VMEM_LIMIT),
        name="mixers",
    )(sinks, bucket_row, rbt, x2, x2, x2, ng, w_mix, cw, cb, lg, lb, mk, mv)


MERGE_TM = 256


def _merge_kernel(x_ref, ng_ref, oa_ref, oc_ref, om_ref, wg_ref, wb_ref, wo_ref, fg_ref, out_ref):
    x = x_ref[...]
    h = _rms(x, ng_ref[...]).astype(BF16)
    y = None
    for n, o_ref in enumerate((oa_ref, oc_ref, om_ref)):
        gate = jax.nn.sigmoid(jnp.dot(h, wg_ref[:, n * D_MODEL:(n + 1) * D_MODEL], preferred_element_type=F32))
        term = gate * jnp.dot(o_ref[...], wb_ref[n], preferred_element_type=F32)
        y = term if y is None else y + term
    xo = x + jnp.dot(y.astype(BF16), wo_ref[...], preferred_element_type=F32)
    out_ref[...] = _rms(xo, fg_ref[...])


def _merge(x2, ng, oa, oc, om, wg, wb, wo, fg):
    t = x2.shape[0]
    tile = pl.BlockSpec((MERGE_TM, D_MODEL), lambda i: (i, 0))
    return pl.pallas_call(
        _merge_kernel,
        grid=(t // MERGE_TM,),
        in_specs=[tile, _resident((1, D_MODEL)), tile, tile, tile,
                  _resident((D_MODEL, N_BRANCHES * D_MODEL)),
                  _resident((N_BRANCHES, D_MODEL, D_MODEL)),
                  _resident((D_MODEL, D_MODEL)),
                  _resident((1, D_MODEL))],
        out_specs=tile,
        out_shape=jax.ShapeDtypeStruct((t, D_MODEL), F32),
        compiler_params=pltpu.CompilerParams(dimension_semantics=("arbitrary",),
                                             vmem_limit_bytes=VMEM_LIMIT),
        name="merge",
    )(x2, ng, oa, oc, om, wg, wb, wo, fg)


def _bucket_row():
    c = jnp.arange(2 * BLOCK)
    d = BLOCK - c
    n = jnp.maximum(d, 0)
    max_exact = NUM_BUCKETS // 2
    nf = jnp.maximum(n, 1).astype(F32)
    large = max_exact + (jnp.log(nf / max_exact) / math.log(MAX_DISTANCE / max_exact)
                         * (NUM_BUCKETS - max_exact)).astype(jnp.int32)
    large = jnp.minimum(large, NUM_BUCKETS - 1)
    bucket = jnp.where(n < max_exact, n, large)
    valid = (d >= 0) & (d < WINDOW)
    return jnp.where(valid, bucket, -1).astype(jnp.int32).reshape(1, 2 * BLOCK)


def kernel(x, mem, rel_bias, norm_g, w_in, sinks, conv_w, conv_b, conv_ln_g, conv_ln_b, mem_norm_g,
           w_mem_kv, w_branch, w_out, final_norm_g):
    nb, s, d = x.shape
    assert d == D_MODEL and w_in.shape[0] == 1 and s % (TILES_PER_STEP * TM) == 0
    x2 = x.reshape(nb * s, d)
    ng = norm_g[0].reshape(1, d)
    w_mix = w_in[0, :, :MIX_WIDTH].astype(BF16)
    w_gate = w_in[0, :, MIX_WIDTH:].astype(BF16)

    mk, mv = _memkv(mem, mem_norm_g[0].reshape(1, d), w_mem_kv[0].astype(BF16))
    oa, oc, om = _mixers(x2, sinks[0], _bucket_row(), rel_bias.T, ng, w_mix, conv_w[0],
                         conv_b[0].reshape(1, d), conv_ln_g[0].reshape(1, d), conv_ln_b[0].reshape(1, d),
                         mk, mv, s)
    out = _merge(x2, ng, oa, oc, om, w_gate, w_branch[0].astype(BF16), w_out[0].astype(BF16),
                 final_norm_g.reshape(1, d))
    return out.reshape(nb, s, d)
```

```python
import functools
import math

import jax
import jax.numpy as jnp
from jax import lax
from jax.experimental import pallas as pl
from jax.experimental.pallas import tpu as pltpu

F32 = jnp.float32
BF16 = jnp.bfloat16

D_MODEL = 1024
N_HEADS = 16
N_KV_HEADS = 2
HEAD_DIM = 64
GROUP = N_HEADS // N_KV_HEADS
WINDOW = 128
BLOCK = 128
NUM_BUCKETS = 32
MAX_DISTANCE = 128
CONV_KERNEL = 31
MEM_LEN = 256
MEM_HEADS = 4
MEM_HEAD_DIM = 256
N_BRANCHES = 3
RMS_EPS = 1e-6
LN_EPS = 1e-5
NEG = -1e30

OFF_QA = 0
OFF_KV = OFF_QA + N_HEADS * HEAD_DIM
OFF_ZA = OFF_KV + 2 * N_KV_HEADS * HEAD_DIM
OFF_UA = OFF_ZA + D_MODEL
OFF_UB = OFF_UA + D_MODEL
OFF_ZC = OFF_UB + D_MODEL
OFF_QM = OFF_ZC + D_MODEL
OFF_ZM = OFF_QM + D_MODEL
OFF_G = OFF_ZM + D_MODEL
MIX_WIDTH = OFF_G

LANES = 128
SUBLANES = 8
HALO = 32
VMEM_LIMIT = 56 * 1024 * 1024

TM = 256
TILES_PER_STEP = 2
CW = 512
SWA_CH = 64
CONV_R = 64
REGIONS_PER_TILE = 8


def _resident(shape):
    return pl.BlockSpec(shape, lambda *_: (0,) * len(shape), pipeline_mode=pl.Buffered(1))


def _silu(z):
    return z * jax.nn.sigmoid(z)


def _rms(x, g):
    ms = jnp.mean(x * x, axis=-1, keepdims=True)
    return x * lax.rsqrt(ms + RMS_EPS) * g


def _memkv_kernel(mem_ref, g_ref, w_ref, mk_ref, mv_ref):
    mn = _rms(mem_ref[0], g_ref[...]).astype(BF16)
    kv = jnp.dot(mn, w_ref[...], preferred_element_type=F32)
    mk_ref[0] = kv[:, :D_MODEL].astype(BF16)
    mv_ref[0] = kv[:, D_MODEL:].astype(BF16)


def _memkv(mem, g, w):
    nb = mem.shape[0]
    return pl.pallas_call(
        _memkv_kernel,
        grid=(nb,),
        in_specs=[pl.BlockSpec((1, MEM_LEN, D_MODEL), lambda b: (b, 0, 0)),
                  _resident((1, D_MODEL)),
                  _resident((D_MODEL, 2 * D_MODEL))],
        out_specs=[pl.BlockSpec((1, MEM_LEN, D_MODEL), lambda b: (b, 0, 0)),
                   pl.BlockSpec((1, MEM_LEN, D_MODEL), lambda b: (b, 0, 0))],
        out_shape=[jax.ShapeDtypeStruct((nb, MEM_LEN, D_MODEL), BF16)] * 2,
        compiler_params=pltpu.CompilerParams(dimension_semantics=("arbitrary",),
                                             vmem_limit_bytes=VMEM_LIMIT),
        name="memkv",
    )(mem, g, w)


def _interleave(fence, n_regions, *lists):
    totals = [sum(c for c, _ in items) for items in lists]
    pos = [0] * len(lists)
    done = [0.0] * len(lists)
    merged = []
    while any(p < len(items) for p, items in zip(pos, lists)):
        live = [i for i, items in enumerate(lists) if pos[i] < len(items)]
        i = min(live, key=lambda i: done[i] / totals[i])
        merged.append(lists[i][pos[i]])
        done[i] += lists[i][pos[i]][0]
        pos[i] += 1
    budget = sum(totals) / n_regions
    group, spent = [], 0.0
    for idx, (cost, fn) in enumerate(merged):
        group.append(fn)
        spent += cost
        if spent >= budget or idx == len(merged) - 1:
            fns = tuple(group)

            @pl.when(fence)
            def _():
                for f in fns:
                    f()
            group, spent = [], 0.0


def _project_items(x_ref, ng_ref, w_ref, slot, halo_fn,
                   h_ref, qa_ref, kv_ref, za_ref, gext_ref, zc_ref, qm_ref, zm_ref):
    items = []

    def proj(off, width):
        return jnp.dot(h_ref[slot], w_ref[:, off:off + width], preferred_element_type=F32)

    def norm():
        h_ref[slot] = _rms(x_ref[...], ng_ref[...]).astype(BF16)
    items.append((300, norm))

    def chunked(dst_ref, off, post):
        for c in range(D_MODEL // CW):
            def fn(c=c):
                dst_ref[slot, :, c * CW:(c + 1) * CW] = post(proj(off + c * CW, CW)).astype(BF16)
            items.append((1024, fn))

    chunked(qa_ref, OFF_QA, lambda v: v * (HEAD_DIM ** -0.5))

    def kv():
        kvf = proj(OFF_KV, 2 * N_KV_HEADS * HEAD_DIM)
        lo = lax.broadcasted_iota(jnp.int32, (TM, LANES), 1) < HEAD_DIM
        for t in range(2):
            xx = kvf[:, t * LANES:(t + 1) * LANES]
            rr = pltpu.roll(xx, HEAD_DIM, 1)
            kv_ref[slot, :, (2 * t) * LANES:(2 * t + 1) * LANES] = jnp.where(lo, xx, rr).astype(BF16)
            kv_ref[slot, :, (2 * t + 1) * LANES:(2 * t + 2) * LANES] = jnp.where(lo, rr, xx).astype(BF16)
    items.append((512, kv))

    chunked(za_ref, OFF_ZA, _silu)

    def halo():
        gext_ref[slot, 0:HALO, :] = halo_fn()
    items.append((10, halo))
    for c in range(D_MODEL // CW):
        def glu(c=c):
            a = proj(OFF_UA + c * CW, CW)
            b = proj(OFF_UB + c * CW, CW)
            gext_ref[slot, HALO:HALO + TM, c * CW:(c + 1) * CW] = a * jax.nn.sigmoid(b)
        items.append((2048, glu))

    chunked(zc_ref, OFF_ZC, _silu)
    chunked(qm_ref, OFF_QM, lambda v: v * (MEM_HEAD_DIM ** -0.5))
    chunked(zm_ref, OFF_ZM, _silu)
    return items


def _mix_items(slot, first, out_rows, sinks_ref, cw_ref, cb_ref, lg_ref, lb_ref, mk_ref, mv_ref,
               oa_ref, oc_ref, om_ref,
               qa_ref, kv_ref, za_ref, gext_ref, zc_ref, qm_ref, zm_ref,
               bias_ref, kvprev_ref, s_ref, p_ref, y_ref):
    items = []
    conv_items = []
    rows_g = GROUP * BLOCK

    def lo_mask():
        return lax.broadcasted_iota(jnp.int32, (BLOCK, LANES), 1) < HEAD_DIM

    for bi in range(TM // BLOCK):
        rows = slice(bi * BLOCK, (bi + 1) * BLOCK)
        variant = first if bi == 0 else 0
        for g in range(N_KV_HEADS):
            kcols = slice(g * LANES, (g + 1) * LANES)
            vcols = slice((N_KV_HEADS + g) * LANES, (N_KV_HEADS + g + 1) * LANES)

            def band(cols, bi=bi, rows=rows):
                if bi == 0:
                    prev = kvprev_ref[:, cols]
                else:
                    prev = kv_ref[slot, (bi - 1) * BLOCK:bi * BLOCK, cols]
                return jnp.concatenate([prev, kv_ref[slot, rows, cols]], axis=0)

            def scores(g=g, rows=rows, kcols=kcols, band=band):
                lo = lo_mask()
                zero = jnp.zeros((), BF16)
                parts = []
                for j in range(GROUP // 2):
                    pair = g * (GROUP // 2) + j
                    qp = qa_ref[slot, rows, pair * LANES:(pair + 1) * LANES]
                    parts.append(jnp.where(lo, qp, zero))
                    parts.append(jnp.where(lo, zero, qp))
                qs = jnp.concatenate(parts, axis=0)
                s_ref[g] = lax.dot_general(qs, band(kcols), (((1,), (1,)), ((), ())),
                                           preferred_element_type=F32)
            items.append((300, scores))

            for ch in range(rows_g // SWA_CH):
                def softmax(g=g, ch=ch, variant=variant):
                    crow = slice(ch * SWA_CH, (ch + 1) * SWA_CH)
                    sink = sinks_ref[g * GROUP + (ch * SWA_CH) // BLOCK]
                    s = s_ref[g, crow, :] + bias_ref[variant, pl.ds(g * rows_g + ch * SWA_CH, SWA_CH), :]
                    m = jnp.maximum(jnp.max(s, axis=-1, keepdims=True), sink)
                    e = jnp.exp(s - m)
                    denom = jnp.sum(e, axis=-1, keepdims=True) + jnp.exp(sink - m)
                    p_ref[g, crow, :] = (e * (1.0 / denom)).astype(BF16)
                items.append((40, softmax))

            def values(g=g, bi=bi, rows=rows, vcols=vcols, band=band):
                lo = lo_mask()
                o = jnp.dot(p_ref[g], band(vcols), preferred_element_type=F32)
                orow = slice(out_rows + bi * BLOCK, out_rows + (bi + 1) * BLOCK)
                for j in range(GROUP // 2):
                    pair = g * (GROUP // 2) + j
                    lanes = slice(pair * LANES, (pair + 1) * LANES)
                    oe = o[(2 * j) * BLOCK:(2 * j + 1) * BLOCK]
                    oo = o[(2 * j + 1) * BLOCK:(2 * j + 2) * BLOCK]
                    oa_ref[orow, lanes] = (jnp.where(lo, oe, oo)
                                           * za_ref[slot, rows, lanes].astype(F32)).astype(BF16)
            items.append((350, values))

    def carry():
        kvprev_ref[...] = kv_ref[slot, TM - BLOCK:TM, :]
    items.append((10, carry))

    off = HALO - (CONV_KERNEL - 1)
    win_rows = CONV_R + HALO
    for c in range(D_MODEL // LANES):
        for r in range(TM // CONV_R):
            def conv(c=c, r=r):
                lanes = slice(c * LANES, (c + 1) * LANES)
                win = gext_ref[slot, r * CONV_R:r * CONV_R + win_rows, lanes]
                acc = jnp.zeros((CONV_R, LANES), F32)
                for res in range(SUBLANES):
                    shifted = win if res == 0 else pltpu.roll(win, win_rows - res, 0)
                    for k in range(CONV_KERNEL):
                        if (off + k) % SUBLANES == res:
                            a = off + k - res
                            acc = acc + cw_ref[k:k + 1, lanes] * shifted[a:a + CONV_R]
                y_ref[r * CONV_R:(r + 1) * CONV_R, lanes] = acc + cb_ref[:, lanes]
            conv_items.append((200, conv))

    for r in range(TM // CONV_R):
        def layer_norm(r=r):
            rows = slice(r * CONV_R, (r + 1) * CONV_R)
            y = y_ref[rows, :]
            mu = jnp.mean(y, axis=-1, keepdims=True)
            yc = y - mu
            var = jnp.mean(yc * yc, axis=-1, keepdims=True)
            ln = yc * lax.rsqrt(var + LN_EPS) * lg_ref[...] + lb_ref[...]
            oc_ref[out_rows + r * CONV_R:out_rows + (r + 1) * CONV_R, :] = (
                _silu(ln) * zc_ref[slot, rows, :].astype(F32)).astype(BF16)
        conv_items.append((400, layer_norm))

    for h in range(MEM_HEADS):
        def mem_head(h=h):
            cols = slice(h * MEM_HEAD_DIM, (h + 1) * MEM_HEAD_DIM)
            s = lax.dot_general(qm_ref[slot, :, cols], mk_ref[0, :, cols], (((1,), (1,)), ((), ())),
                                preferred_element_type=F32)
            m = jnp.max(s, axis=-1, keepdims=True)
            e = jnp.exp(s - m)
            p = (e * (1.0 / jnp.sum(e, axis=-1, keepdims=True))).astype(BF16)
            o = jnp.dot(p, mv_ref[0, :, cols], preferred_element_type=F32)
            om_ref[out_rows:out_rows + TM, cols] = (o * zm_ref[slot, :, cols].astype(F32)).astype(BF16)
        items.append((450, mem_head))
    return items, conv_items


def _mixers_kernel(fence_ref, sinks_ref, bucket_ref, rbt_ref, x0_ref, xa_ref, xb_ref, ng_ref, w_ref,
                   cw_ref, cb_ref, lg_ref, lb_ref, mk_ref, mv_ref,
                   oa_ref, oc_ref, om_ref,
                   h_ref, qa_ref, kv_ref, za_ref, gext_ref, zc_ref, qm_ref, zm_ref,
                   bias_ref, kvprev_ref, s_ref, p_ref, y_ref, *, tiles_per_seq):
    j = pl.program_id(0)
    stage = (h_ref, qa_ref, kv_ref, za_ref, gext_ref, zc_ref, qm_ref, zm_ref)
    mixargs = (sinks_ref, cw_ref, cb_ref, lg_ref, lb_ref, mk_ref, mv_ref, oa_ref, oc_ref, om_ref,
               qa_ref, kv_ref, za_ref, gext_ref, zc_ref, qm_ref, zm_ref,
               bias_ref, kvprev_ref, s_ref, p_ref, y_ref)

    def zero_halo():
        return jnp.zeros((HALO, D_MODEL), F32)

    @pl.when(j == 0)
    def _():
        bucket = bucket_ref[...]
        rbt = rbt_ref[...]
        r0 = jnp.full((N_HEADS, 2 * BLOCK), NEG, F32)
        for bk in range(NUM_BUCKETS):
            r0 = jnp.where(bucket == bk, rbt[:, bk:bk + 1], r0)
        col = lax.broadcasted_iota(jnp.int32, (BLOCK, 2 * BLOCK), 1)
        for h in range(N_HEADS):
            t = jnp.broadcast_to(r0[h:h + 1, :], (BLOCK, 2 * BLOCK))
            t = pltpu.roll(t, 0, 1, stride=1, stride_axis=0)
            bias_ref[0, h * BLOCK:(h + 1) * BLOCK, :] = t
            bias_ref[1, h * BLOCK:(h + 1) * BLOCK, :] = jnp.where(col < BLOCK, NEG, t)
        kvprev_ref[...] = jnp.zeros(kvprev_ref.shape, BF16)
        for _, fn in _project_items(x0_ref, ng_ref, w_ref, 0, zero_halo, *stage):
            fn()

    steps_per_seq = tiles_per_seq // TILES_PER_STEP
    seq_start = (j % steps_per_seq) == 0
    next_seq_start = (j % steps_per_seq) == steps_per_seq - 1

    def halo_from(slot, zero_if=None):
        def fn():
            tail = gext_ref[slot, TM:TM + HALO, :]
            return tail if zero_if is None else jnp.where(zero_if, zero_halo(), tail)
        return fn

    fence = fence_ref[0] != 0
    _interleave(fence, REGIONS_PER_TILE, *_mix_items(0, jnp.where(seq_start, 1, 0), 0, *mixargs),
                _project_items(xa_ref, ng_ref, w_ref, 1, halo_from(0), *stage))
    _interleave(fence, REGIONS_PER_TILE, *_mix_items(1, 0, TM, *mixargs),
                _project_items(xb_ref, ng_ref, w_ref, 0, halo_from(1, next_seq_start), *stage))


def _mixers(x2, sinks, bucket_row, rbt, ng, w_mix, cw, cb, lg, lb, mk, mv, s):
    t = x2.shape[0]
    n_tiles = t // TM
    tiles_per_seq = s // TM
    steps_per_seq = tiles_per_seq // TILES_PER_STEP
    xspec = lambda k: pl.BlockSpec(
        (TM, D_MODEL), lambda j: (jnp.minimum(TILES_PER_STEP * j + k, n_tiles - 1), 0))
    memspec = pl.BlockSpec((1, MEM_LEN, D_MODEL), lambda j: (j // steps_per_seq, 0, 0))
    ospec = pl.BlockSpec((TILES_PER_STEP * TM, D_MODEL), lambda j: (j, 0))
    return pl.pallas_call(
        functools.partial(_mixers_kernel, tiles_per_seq=tiles_per_seq),
        grid=(n_tiles // TILES_PER_STEP,),
        in_specs=[pl.BlockSpec(memory_space=pltpu.SMEM),
                  pl.BlockSpec(memory_space=pltpu.SMEM),
                  _resident((1, 2 * BLOCK)),
                  _resident((N_HEADS, NUM_BUCKETS)),
                  pl.BlockSpec((TM, D_MODEL), lambda j: (0, 0), pipeline_mode=pl.Buffered(1)),
                  xspec(1), xspec(2),
                  _resident((1, D_MODEL)),
                  _resident((D_MODEL, MIX_WIDTH)),
                  _resident((CONV_KERNEL, D_MODEL)),
                  _resident((1, D_MODEL)), _resident((1, D_MODEL)), _resident((1, D_MODEL)),
                  memspec, memspec],
        out_specs=[ospec, ospec, ospec],
        out_shape=[jax.ShapeDtypeStruct((t, D_MODEL), BF16)] * 3,
        scratch_shapes=[pltpu.VMEM((2, TM, D_MODEL), BF16),
                        pltpu.VMEM((2, TM, D_MODEL), BF16),
                        pltpu.VMEM((2, TM, 4 * LANES), BF16),
                        pltpu.VMEM((2, TM, D_MODEL), BF16),
                        pltpu.VMEM((2, TM + HALO, D_MODEL), F32),
                        pltpu.VMEM((2, TM, D_MODEL), BF16),
                        pltpu.VMEM((2, TM, D_MODEL), BF16),
                        pltpu.VMEM((2, TM, D_MODEL), BF16),
                        pltpu.VMEM((2, N_HEADS * BLOCK, 2 * BLOCK), F32),
                        pltpu.VMEM((BLOCK, 4 * LANES), BF16),
                        pltpu.VMEM((N_KV_HEADS, GROUP * BLOCK, 2 * BLOCK), F32),
                        pltpu.VMEM((N_KV_HEADS, GROUP * BLOCK, 2 * BLOCK), BF16),
                        pltpu.VMEM((TM, D_MODEL), F32)],
        compiler_params=pltpu.CompilerParams(dimension_semantics=("arbitrary",),
                                             vmem_limit_bytes=VMEM_LIMIT),
        name="mixers",
    )(jnp.ones((1,), jnp.int32), sinks, bucket_row, rbt, x2, x2, x2, ng, w_mix, cw, cb, lg, lb, mk, mv)


MERGE_TM = 256


def _merge_kernel(x_ref, ng_ref, oa_ref, oc_ref, om_ref, wg_ref, wb_ref, wo_ref, fg_ref, out_ref):
    x = x_ref[...]
    h = _rms(x, ng_ref[...]).astype(BF16)
    y = None
    for n, o_ref in enumerate((oa_ref, oc_ref, om_ref)):
        gate = jax.nn.sigmoid(jnp.dot(h, wg_ref[:, n * D_MODEL:(n + 1) * D_MODEL], preferred_element_type=F32))
        term = gate * jnp.dot(o_ref[...], wb_ref[n], preferred_element_type=F32)
        y = term if y is None else y + term
    xo = x + jnp.dot(y.astype(BF16), wo_ref[...], preferred_element_type=F32)
    out_ref[...] = _rms(xo, fg_ref[...])


def _merge(x2, ng, oa, oc, om, wg, wb, wo, fg):
    t = x2.shape[0]
    tile = pl.BlockSpec((MERGE_TM, D_MODEL), lambda i: (i, 0))
    return pl.pallas_call(
        _merge_kernel,
        grid=(t // MERGE_TM,),
        in_specs=[tile, _resident((1, D_MODEL)), tile, tile, tile,
                  _resident((D_MODEL, N_BRANCHES * D_MODEL)),
                  _resident((N_BRANCHES, D_MODEL, D_MODEL)),
                  _resident((D_MODEL, D_MODEL)),
                  _resident((1, D_MODEL))],
        out_specs=tile,
        out_shape=jax.ShapeDtypeStruct((t, D_MODEL), F32),
        compiler_params=pltpu.CompilerParams(dimension_semantics=("arbitrary",),
                                             vmem_limit_bytes=VMEM_LIMIT),
        name="merge",
    )(x2, ng, oa, oc, om, wg, wb, wo, fg)


def _bucket_row():
    c = jnp.arange(2 * BLOCK)
    d = BLOCK - c
    n = jnp.maximum(d, 0)
    max_exact = NUM_BUCKETS // 2
    nf = jnp.maximum(n, 1).astype(F32)
    large = max_exact + (jnp.log(nf / max_exact) / math.log(MAX_DISTANCE / max_exact)
                         * (NUM_BUCKETS - max_exact)).astype(jnp.int32)
    large = jnp.minimum(large, NUM_BUCKETS - 1)
    bucket = jnp.where(n < max_exact, n, large)
    valid = (d >= 0) & (d < WINDOW)
    return jnp.where(valid, bucket, -1).astype(jnp.int32).reshape(1, 2 * BLOCK)


def kernel(x, mem, rel_bias, norm_g, w_in, sinks, conv_w, conv_b, conv_ln_g, conv_ln_b, mem_norm_g,
           w_mem_kv, w_branch, w_out, final_norm_g):
    nb, s, d = x.shape
    assert d == D_MODEL and w_in.shape[0] == 1 and s % (TILES_PER_STEP * TM) == 0
    x2 = x.reshape(nb * s, d)
    ng = norm_g[0].reshape(1, d)
    w_mix = w_in[0, :, :MIX_WIDTH].astype(BF16)
    w_gate = w_in[0, :, MIX_WIDTH:].astype(BF16)

    mk, mv = _memkv(mem, mem_norm_g[0].reshape(1, d), w_mem_kv[0].astype(BF16))
    oa, oc, om = _mixers(x2, sinks[0], _bucket_row(), rel_bias.T, ng, w_mix, conv_w[0],
                         conv_b[0].reshape(1, d), conv_ln_g[0].reshape(1, d), conv_ln_b[0].reshape(1, d),
                         mk, mv, s)
    out = _merge(x2, ng, oa, oc, om, w_gate, w_branch[0].astype(BF16), w_out[0].astype(BF16),
                 final_norm_g.reshape(1, d))
    return out.reshape(nb, s, d)
```

```python
import functools
import math

import jax
import jax.numpy as jnp
from jax import lax
from jax.experimental import pallas as pl
from jax.experimental.pallas import tpu as pltpu

F32 = jnp.float32
BF16 = jnp.bfloat16

D_MODEL = 1024
N_HEADS = 16
N_KV_HEADS = 2
HEAD_DIM = 64
GROUP = N_HEADS // N_KV_HEADS
WINDOW = 128
BLOCK = 128
NUM_BUCKETS = 32
MAX_DISTANCE = 128
CONV_KERNEL = 31
MEM_LEN = 256
MEM_HEADS = 4
MEM_HEAD_DIM = 256
N_BRANCHES = 3
RMS_EPS = 1e-6
LN_EPS = 1e-5
NEG = -1e30

OFF_QA = 0
OFF_KV = OFF_QA + N_HEADS * HEAD_DIM
OFF_ZA = OFF_KV + 2 * N_KV_HEADS * HEAD_DIM
OFF_UA = OFF_ZA + D_MODEL
OFF_UB = OFF_UA + D_MODEL
OFF_ZC = OFF_UB + D_MODEL
OFF_QM = OFF_ZC + D_MODEL
OFF_ZM = OFF_QM + D_MODEL
OFF_G = OFF_ZM + D_MODEL
MIX_WIDTH = OFF_G

LANES = 128
SUBLANES = 8
HALO = 32
VMEM_LIMIT = 56 * 1024 * 1024

TM = 256
TILES_PER_STEP = 2
CW = 512
SWA_CH = 64
CONV_R = 64
REGIONS_PER_TILE = 8


def _resident(shape):
    return pl.BlockSpec(shape, lambda *_: (0,) * len(shape), pipeline_mode=pl.Buffered(1))


def _silu(z):
    return z * jax.nn.sigmoid(z)


def _rms(x, g):
    ms = jnp.mean(x * x, axis=-1, keepdims=True)
    return x * lax.rsqrt(ms + RMS_EPS) * g


def _memkv_kernel(mem_ref, g_ref, w_ref, mk_ref, mv_ref):
    mn = _rms(mem_ref[0], g_ref[...]).astype(BF16)
    kv = jnp.dot(mn, w_ref[...], preferred_element_type=F32)
    mk_ref[0] = kv[:, :D_MODEL].astype(BF16)
    mv_ref[0] = kv[:, D_MODEL:].astype(BF16)


def _memkv(mem, g, w):
    nb = mem.shape[0]
    return pl.pallas_call(
        _memkv_kernel,
        grid=(nb,),
        in_specs=[pl.BlockSpec((1, MEM_LEN, D_MODEL), lambda b: (b, 0, 0)),
                  _resident((1, D_MODEL)),
                  _resident((D_MODEL, 2 * D_MODEL))],
        out_specs=[pl.BlockSpec((1, MEM_LEN, D_MODEL), lambda b: (b, 0, 0)),
                   pl.BlockSpec((1, MEM_LEN, D_MODEL), lambda b: (b, 0, 0))],
        out_shape=[jax.ShapeDtypeStruct((nb, MEM_LEN, D_MODEL), BF16)] * 2,
        compiler_params=pltpu.CompilerParams(dimension_semantics=("arbitrary",),
                                             vmem_limit_bytes=VMEM_LIMIT),
        name="memkv",
    )(mem, g, w)


def _interleave(fence_ref, n_regions, *lists):
    totals = [sum(c for c, _ in items) for items in lists]
    pos = [0] * len(lists)
    done = [0.0] * len(lists)
    merged = []
    while any(p < len(items) for p, items in zip(pos, lists)):
        live = [i for i, items in enumerate(lists) if pos[i] < len(items)]
        i = min(live, key=lambda i: done[i] / totals[i])
        merged.append(lists[i][pos[i]])
        done[i] += lists[i][pos[i]][0]
        pos[i] += 1
    budget = sum(totals) / n_regions
    group, spent, region = [], 0.0, 0
    for idx, (cost, fn) in enumerate(merged):
        group.append(fn)
        spent += cost
        if spent >= budget or idx == len(merged) - 1:
            fns = tuple(group)

            @pl.when(fence_ref[region] != 0)
            def _():
                for f in fns:
                    f()
            group, spent, region = [], 0.0, region + 1


def _project_items(x_ref, ng_ref, w_ref, slot, halo_fn,
                   h_ref, qa_ref, kv_ref, za_ref, gext_ref, zc_ref, qm_ref, zm_ref):
    items = []

    def proj(off, width):
        return jnp.dot(h_ref[slot], w_ref[:, off:off + width], preferred_element_type=F32)

    def norm():
        h_ref[slot] = _rms(x_ref[...], ng_ref[...]).astype(BF16)
    items.append((300, norm))

    def chunked(dst_ref, off, post):
        for c in range(D_MODEL // CW):
            def fn(c=c):
                dst_ref[slot, :, c * CW:(c + 1) * CW] = post(proj(off + c * CW, CW)).astype(BF16)
            items.append((1024, fn))

    chunked(qa_ref, OFF_QA, lambda v: v * (HEAD_DIM ** -0.5))

    def kv():
        kvf = proj(OFF_KV, 2 * N_KV_HEADS * HEAD_DIM)
        lo = lax.broadcasted_iota(jnp.int32, (TM, LANES), 1) < HEAD_DIM
        for t in range(2):
            xx = kvf[:, t * LANES:(t + 1) * LANES]
            rr = pltpu.roll(xx, HEAD_DIM, 1)
            kv_ref[slot, :, (2 * t) * LANES:(2 * t + 1) * LANES] = jnp.where(lo, xx, rr).astype(BF16)
            kv_ref[slot, :, (2 * t + 1) * LANES:(2 * t + 2) * LANES] = jnp.where(lo, rr, xx).astype(BF16)
    items.append((512, kv))

    chunked(za_ref, OFF_ZA, _silu)

    def halo():
        gext_ref[slot, 0:HALO, :] = halo_fn()
    items.append((10, halo))
    for c in range(D_MODEL // CW):
        def glu(c=c):
            a = proj(OFF_UA + c * CW, CW)
            b = proj(OFF_UB + c * CW, CW)
            gext_ref[slot, HALO:HALO + TM, c * CW:(c + 1) * CW] = a * jax.nn.sigmoid(b)
        items.append((2048, glu))

    chunked(zc_ref, OFF_ZC, _silu)
    chunked(qm_ref, OFF_QM, lambda v: v * (MEM_HEAD_DIM ** -0.5))
    chunked(zm_ref, OFF_ZM, _silu)
    return items


def _mix_items(slot, first, out_rows, sinks_ref, cw_ref, cb_ref, lg_ref, lb_ref, mk_ref, mv_ref,
               oa_ref, oc_ref, om_ref,
               qa_ref, kv_ref, za_ref, gext_ref, zc_ref, qm_ref, zm_ref,
               bias_ref, kvprev_ref, s_ref, p_ref, y_ref):
    items = []
    conv_items = []
    rows_g = GROUP * BLOCK

    def lo_mask():
        return lax.broadcasted_iota(jnp.int32, (BLOCK, LANES), 1) < HEAD_DIM

    for bi in range(TM // BLOCK):
        rows = slice(bi * BLOCK, (bi + 1) * BLOCK)
        variant = first if bi == 0 else 0
        for g in range(N_KV_HEADS):
            kcols = slice(g * LANES, (g + 1) * LANES)
            vcols = slice((N_KV_HEADS + g) * LANES, (N_KV_HEADS + g + 1) * LANES)

            def band(cols, bi=bi, rows=rows):
                if bi == 0:
                    prev = kvprev_ref[:, cols]
                else:
                    prev = kv_ref[slot, (bi - 1) * BLOCK:bi * BLOCK, cols]
                return jnp.concatenate([prev, kv_ref[slot, rows, cols]], axis=0)

            def scores(g=g, rows=rows, kcols=kcols, band=band):
                lo = lo_mask()
                zero = jnp.zeros((), BF16)
                parts = []
                for j in range(GROUP // 2):
                    pair = g * (GROUP // 2) + j
                    qp = qa_ref[slot, rows, pair * LANES:(pair + 1) * LANES]
                    parts.append(jnp.where(lo, qp, zero))
                    parts.append(jnp.where(lo, zero, qp))
                qs = jnp.concatenate(parts, axis=0)
                s_ref[g] = lax.dot_general(qs, band(kcols), (((1,), (1,)), ((), ())),
                                           preferred_element_type=F32)
            items.append((300, scores))

            for ch in range(rows_g // SWA_CH):
                def softmax(g=g, ch=ch, variant=variant):
                    crow = slice(ch * SWA_CH, (ch + 1) * SWA_CH)
                    sink = sinks_ref[g * GROUP + (ch * SWA_CH) // BLOCK]
                    s = s_ref[g, crow, :] + bias_ref[variant, pl.ds(g * rows_g + ch * SWA_CH, SWA_CH), :]
                    m = jnp.maximum(jnp.max(s, axis=-1, keepdims=True), sink)
                    e = jnp.exp(s - m)
                    denom = jnp.sum(e, axis=-1, keepdims=True) + jnp.exp(sink - m)
                    p_ref[g, crow, :] = (e * (1.0 / denom)).astype(BF16)
                items.append((40, softmax))

            def values(g=g, bi=bi, rows=rows, vcols=vcols, band=band):
                lo = lo_mask()
                o = jnp.dot(p_ref[g], band(vcols), preferred_element_type=F32)
                orow = slice(out_rows + bi * BLOCK, out_rows + (bi + 1) * BLOCK)
                for j in range(GROUP // 2):
                    pair = g * (GROUP // 2) + j
                    lanes = slice(pair * LANES, (pair + 1) * LANES)
                    oe = o[(2 * j) * BLOCK:(2 * j + 1) * BLOCK]
                    oo = o[(2 * j + 1) * BLOCK:(2 * j + 2) * BLOCK]
                    oa_ref[orow, lanes] = (jnp.where(lo, oe, oo)
                                           * za_ref[slot, rows, lanes].astype(F32)).astype(BF16)
            items.append((350, values))

    def carry():
        kvprev_ref[...] = kv_ref[slot, TM - BLOCK:TM, :]
    items.append((10, carry))

    off = HALO - (CONV_KERNEL - 1)
    win_rows = CONV_R + HALO
    for c in range(D_MODEL // LANES):
        for r in range(TM // CONV_R):
            def conv(c=c, r=r):
                lanes = slice(c * LANES, (c + 1) * LANES)
                win = gext_ref[slot, r * CONV_R:r * CONV_R + win_rows, lanes]
                acc = jnp.zeros((CONV_R, LANES), F32)
                for res in range(SUBLANES):
                    shifted = win if res == 0 else pltpu.roll(win, win_rows - res, 0)
                    for k in range(CONV_KERNEL):
                        if (off + k) % SUBLANES == res:
                            a = off + k - res
                            acc = acc + cw_ref[k:k + 1, lanes] * shifted[a:a + CONV_R]
                y_ref[r * CONV_R:(r + 1) * CONV_R, lanes] = acc + cb_ref[:, lanes]
            conv_items.append((200, conv))

    for r in range(TM // CONV_R):
        def layer_norm(r=r):
            rows = slice(r * CONV_R, (r + 1) * CONV_R)
            y = y_ref[rows, :]
            mu = jnp.mean(y, axis=-1, keepdims=True)
            yc = y - mu
            var = jnp.mean(yc * yc, axis=-1, keepdims=True)
            ln = yc * lax.rsqrt(var + LN_EPS) * lg_ref[...] + lb_ref[...]
            oc_ref[out_rows + r * CONV_R:out_rows + (r + 1) * CONV_R, :] = (
                _silu(ln) * zc_ref[slot, rows, :].astype(F32)).astype(BF16)
        conv_items.append((400, layer_norm))

    for h in range(MEM_HEADS):
        def mem_head(h=h):
            cols = slice(h * MEM_HEAD_DIM, (h + 1) * MEM_HEAD_DIM)
            s = lax.dot_general(qm_ref[slot, :, cols], mk_ref[0, :, cols], (((1,), (1,)), ((), ())),
                                preferred_element_type=F32)
            m = jnp.max(s, axis=-1, keepdims=True)
            e = jnp.exp(s - m)
            p = (e * (1.0 / jnp.sum(e, axis=-1, keepdims=True))).astype(BF16)
            o = jnp.dot(p, mv_ref[0, :, cols], preferred_element_type=F32)
            om_ref[out_rows:out_rows + TM, cols] = (o * zm_ref[slot, :, cols].astype(F32)).astype(BF16)
        items.append((450, mem_head))
    return items, conv_items


def _mixers_kernel(fence_ref, sinks_ref, bucket_ref, rbt_ref, x0_ref, xa_ref, xb_ref, ng_ref, w_ref,
                   cw_ref, cb_ref, lg_ref, lb_ref, mk_ref, mv_ref,
                   oa_ref, oc_ref, om_ref,
                   h_ref, qa_ref, kv_ref, za_ref, gext_ref, zc_ref, qm_ref, zm_ref,
                   bias_ref, kvprev_ref, s_ref, p_ref, y_ref, *, tiles_per_seq):
    j = pl.program_id(0)
    stage = (h_ref, qa_ref, kv_ref, za_ref, gext_ref, zc_ref, qm_ref, zm_ref)
    mixargs = (sinks_ref, cw_ref, cb_ref, lg_ref, lb_ref, mk_ref, mv_ref, oa_ref, oc_ref, om_ref,
               qa_ref, kv_ref, za_ref, gext_ref, zc_ref, qm_ref, zm_ref,
               bias_ref, kvprev_ref, s_ref, p_ref, y_ref)

    def zero_halo():
        return jnp.zeros((HALO, D_MODEL), F32)

    @pl.when(j == 0)
    def _():
        bucket = bucket_ref[...]
        rbt = rbt_ref[...]
        r0 = jnp.full((N_HEADS, 2 * BLOCK), NEG, F32)
        for bk in range(NUM_BUCKETS):
            r0 = jnp.where(bucket == bk, rbt[:, bk:bk + 1], r0)
        col = lax.broadcasted_iota(jnp.int32, (BLOCK, 2 * BLOCK), 1)
        for h in range(N_HEADS):
            t = jnp.broadcast_to(r0[h:h + 1, :], (BLOCK, 2 * BLOCK))
            t = pltpu.roll(t, 0, 1, stride=1, stride_axis=0)
            bias_ref[0, h * BLOCK:(h + 1) * BLOCK, :] = t
            bias_ref[1, h * BLOCK:(h + 1) * BLOCK, :] = jnp.where(col < BLOCK, NEG, t)
        kvprev_ref[...] = jnp.zeros(kvprev_ref.shape, BF16)
        for _, fn in _project_items(x0_ref, ng_ref, w_ref, 0, zero_halo, *stage):
            fn()

    steps_per_seq = tiles_per_seq // TILES_PER_STEP
    seq_start = (j % steps_per_seq) == 0
    next_seq_start = (j % steps_per_seq) == steps_per_seq - 1

    def halo_from(slot, zero_if=None):
        def fn():
            tail = gext_ref[slot, TM:TM + HALO, :]
            return tail if zero_if is None else jnp.where(zero_if, zero_halo(), tail)
        return fn

    _interleave(fence_ref, REGIONS_PER_TILE, *_mix_items(0, jnp.where(seq_start, 1, 0), 0, *mixargs),
                _project_items(xa_ref, ng_ref, w_ref, 1, halo_from(0), *stage))
    _interleave(fence_ref, REGIONS_PER_TILE, *_mix_items(1, 0, TM, *mixargs),
                _project_items(xb_ref, ng_ref, w_ref, 0, halo_from(1, next_seq_start), *stage))


def _mixers(x2, sinks, bucket_row, rbt, ng, w_mix, cw, cb, lg, lb, mk, mv, s):
    t = x2.shape[0]
    n_tiles = t // TM
    tiles_per_seq = s // TM
    steps_per_seq = tiles_per_seq // TILES_PER_STEP
    xspec = lambda k: pl.BlockSpec(
        (TM, D_MODEL), lambda j: (jnp.minimum(TILES_PER_STEP * j + k, n_tiles - 1), 0))
    memspec = pl.BlockSpec((1, MEM_LEN, D_MODEL), lambda j: (j // steps_per_seq, 0, 0))
    ospec = pl.BlockSpec((TILES_PER_STEP * TM, D_MODEL), lambda j: (j, 0))
    return pl.pallas_call(
        functools.partial(_mixers_kernel, tiles_per_seq=tiles_per_seq),
        grid=(n_tiles // TILES_PER_STEP,),
        in_specs=[pl.BlockSpec(memory_space=pltpu.SMEM),
                  pl.BlockSpec(memory_space=pltpu.SMEM),
                  _resident((1, 2 * BLOCK)),
                  _resident((N_HEADS, NUM_BUCKETS)),
                  pl.BlockSpec((TM, D_MODEL), lambda j: (0, 0), pipeline_mode=pl.Buffered(1)),
                  xspec(1), xspec(2),
                  _resident((1, D_MODEL)),
                  _resident((D_MODEL, MIX_WIDTH)),
                  _resident((CONV_KERNEL, D_MODEL)),
                  _resident((1, D_MODEL)), _resident((1, D_MODEL)), _resident((1, D_MODEL)),
                  memspec, memspec],
        out_specs=[ospec, ospec, ospec],
        out_shape=[jax.ShapeDtypeStruct((t, D_MODEL), BF16)] * 3,
        scratch_shapes=[pltpu.VMEM((2, TM, D_MODEL), BF16),
                        pltpu.VMEM((2, TM, D_MODEL), BF16),
                        pltpu.VMEM((2, TM, 4 * LANES), BF16),
                        pltpu.VMEM((2, TM, D_MODEL), BF16),
                        pltpu.VMEM((2, TM + HALO, D_MODEL), F32),
                        pltpu.VMEM((2, TM, D_MODEL), BF16),
                        pltpu.VMEM((2, TM, D_MODEL), BF16),
                        pltpu.VMEM((2, TM, D_MODEL), BF16),
                        pltpu.VMEM((2, N_HEADS * BLOCK, 2 * BLOCK), F32),
                        pltpu.VMEM((BLOCK, 4 * LANES), BF16),
                        pltpu.VMEM((N_KV_HEADS, GROUP * BLOCK, 2 * BLOCK), F32),
                        pltpu.VMEM((N_KV_HEADS, GROUP * BLOCK, 2 * BLOCK), BF16),
                        pltpu.VMEM((TM, D_MODEL), F32)],
        compiler_params=pltpu.CompilerParams(dimension_semantics=("arbitrary",),
                                             vmem_limit_bytes=VMEM_LIMIT),
        name="mixers",
    )(jnp.ones((REGIONS_PER_TILE + 1,), jnp.int32), sinks, bucket_row, rbt, x2, x2, x2, ng, w_mix, cw, cb, lg, lb, mk, mv)


MERGE_TM = 256


def _merge_kernel(x_ref, ng_ref, oa_ref, oc_ref, om_ref, wg_ref, wb_ref, wo_ref, fg_ref, out_ref):
    x = x_ref[...]
    h = _rms(x, ng_ref[...]).astype(BF16)
    y = None
    for n, o_ref in enumerate((oa_ref, oc_ref, om_ref)):
        gate = jax.nn.sigmoid(jnp.dot(h, wg_ref[:, n * D_MODEL:(n + 1) * D_MODEL], preferred_element_type=F32))
        term = gate * jnp.dot(o_ref[...], wb_ref[n], preferred_element_type=F32)
        y = term if y is None else y + term
    xo = x + jnp.dot(y.astype(BF16), wo_ref[...], preferred_element_type=F32)
    out_ref[...] = _rms(xo, fg_ref[...])


def _merge(x2, ng, oa, oc, om, wg, wb, wo, fg):
    t = x2.shape[0]
    tile = pl.BlockSpec((MERGE_TM, D_MODEL), lambda i: (i, 0))
    return pl.pallas_call(
        _merge_kernel,
        grid=(t // MERGE_TM,),
        in_specs=[tile, _resident((1, D_MODEL)), tile, tile, tile,
                  _resident((D_MODEL, N_BRANCHES * D_MODEL)),
                  _resident((N_BRANCHES, D_MODEL, D_MODEL)),
                  _resident((D_MODEL, D_MODEL)),
                  _resident((1, D_MODEL))],
        out_specs=tile,
        out_shape=jax.ShapeDtypeStruct((t, D_MODEL), F32),
        compiler_params=pltpu.CompilerParams(dimension_semantics=("arbitrary",),
                                             vmem_limit_bytes=VMEM_LIMIT),
        name="merge",
    )(x2, ng, oa, oc, om, wg, wb, wo, fg)


def _bucket_row():
    c = jnp.arange(2 * BLOCK)
    d = BLOCK - c
    n = jnp.maximum(d, 0)
    max_exact = NUM_BUCKETS // 2
    nf = jnp.maximum(n, 1).astype(F32)
    large = max_exact + (jnp.log(nf / max_exact) / math.log(MAX_DISTANCE / max_exact)
                         * (NUM_BUCKETS - max_exact)).astype(jnp.int32)
    large = jnp.minimum(large, NUM_BUCKETS - 1)
    bucket = jnp.where(n < max_exact, n, large)
    valid = (d >= 0) & (d < WINDOW)
    return jnp.where(valid, bucket, -1).astype(jnp.int32).reshape(1, 2 * BLOCK)


def kernel(x, mem, rel_bias, norm_g, w_in, sinks, conv_w, conv_b, conv_ln_g, conv_ln_b, mem_norm_g,
           w_mem_kv, w_branch, w_out, final_norm_g):
    nb, s, d = x.shape
    assert d == D_MODEL and w_in.shape[0] == 1 and s % (TILES_PER_STEP * TM) == 0
    x2 = x.reshape(nb * s, d)
    ng = norm_g[0].reshape(1, d)
    w_mix = w_in[0, :, :MIX_WIDTH].astype(BF16)
    w_gate = w_in[0, :, MIX_WIDTH:].astype(BF16)

    mk, mv = _memkv(mem, mem_norm_g[0].reshape(1, d), w_mem_kv[0].astype(BF16))
    oa, oc, om = _mixers(x2, sinks[0], _bucket_row(), rel_bias.T, ng, w_mix, conv_w[0],
                         conv_b[0].reshape(1, d), conv_ln_g[0].reshape(1, d), conv_ln_b[0].reshape(1, d),
                         mk, mv, s)
    out = _merge(x2, ng, oa, oc, om, w_gate, w_branch[0].astype(BF16), w_out[0].astype(BF16),
                 final_norm_g.reshape(1, d))
    return out.reshape(nb, s, d)
```

```python
import functools
import math

import jax
import jax.numpy as jnp
from jax import lax
from jax.experimental import pallas as pl
from jax.experimental.pallas import tpu as pltpu

F32 = jnp.float32
BF16 = jnp.bfloat16

D_MODEL = 1024
N_HEADS = 16
N_KV_HEADS = 2
HEAD_DIM = 64
GROUP = N_HEADS // N_KV_HEADS
WINDOW = 128
BLOCK = 128
NUM_BUCKETS = 32
MAX_DISTANCE = 128
CONV_KERNEL = 31
MEM_LEN = 256
MEM_HEADS = 4
MEM_HEAD_DIM = 256
N_BRANCHES = 3
RMS_EPS = 1e-6
LN_EPS = 1e-5
NEG = -1e30

OFF_QA = 0
OFF_KV = OFF_QA + N_HEADS * HEAD_DIM
OFF_ZA = OFF_KV + 2 * N_KV_HEADS * HEAD_DIM
OFF_UA = OFF_ZA + D_MODEL
OFF_UB = OFF_UA + D_MODEL
OFF_ZC = OFF_UB + D_MODEL
OFF_QM = OFF_ZC + D_MODEL
OFF_ZM = OFF_QM + D_MODEL
OFF_G = OFF_ZM + D_MODEL
MIX_WIDTH = OFF_G

LANES = 128
SUBLANES = 8
HALO = 32
VMEM_LIMIT = 56 * 1024 * 1024

TM = 256
TILES_PER_STEP = 2
CW = 512
SWA_CH = 64
CONV_R = 64


def _resident(shape):
    return pl.BlockSpec(shape, lambda *_: (0,) * len(shape), pipeline_mode=pl.Buffered(1))


def _silu(z):
    return z * jax.nn.sigmoid(z)


def _rms(x, g):
    ms = jnp.mean(x * x, axis=-1, keepdims=True)
    return x * lax.rsqrt(ms + RMS_EPS) * g


def _memkv_kernel(mem_ref, g_ref, w_ref, mk_ref, mv_ref):
    mn = _rms(mem_ref[0], g_ref[...]).astype(BF16)
    kv = jnp.dot(mn, w_ref[...], preferred_element_type=F32)
    mk_ref[0] = kv[:, :D_MODEL].astype(BF16)
    mv_ref[0] = kv[:, D_MODEL:].astype(BF16)


def _memkv(mem, g, w):
    nb = mem.shape[0]
    return pl.pallas_call(
        _memkv_kernel,
        grid=(nb,),
        in_specs=[pl.BlockSpec((1, MEM_LEN, D_MODEL), lambda b: (b, 0, 0)),
                  _resident((1, D_MODEL)),
                  _resident((D_MODEL, 2 * D_MODEL))],
        out_specs=[pl.BlockSpec((1, MEM_LEN, D_MODEL), lambda b: (b, 0, 0)),
                   pl.BlockSpec((1, MEM_LEN, D_MODEL), lambda b: (b, 0, 0))],
        out_shape=[jax.ShapeDtypeStruct((nb, MEM_LEN, D_MODEL), BF16)] * 2,
        compiler_params=pltpu.CompilerParams(dimension_semantics=("arbitrary",),
                                             vmem_limit_bytes=VMEM_LIMIT),
        name="memkv",
    )(mem, g, w)


def _interleave(*lists):
    totals = [sum(c for c, _ in items) for items in lists]
    pos = [0] * len(lists)
    done = [0.0] * len(lists)
    while any(p < len(items) for p, items in zip(pos, lists)):
        live = [i for i, items in enumerate(lists) if pos[i] < len(items)]
        i = min(live, key=lambda i: done[i] / totals[i])
        cost, fn = lists[i][pos[i]]
        pos[i] += 1
        done[i] += cost
        fn()


def _project_items(x_ref, ng_ref, w_ref, slot, halo_fn,
                   h_ref, qa_ref, kv_ref, za_ref, gext_ref, zc_ref, qm_ref, zm_ref):
    items = []

    def proj(off, width):
        return jnp.dot(h_ref[slot], w_ref[:, off:off + width], preferred_element_type=F32)

    def norm():
        h_ref[slot] = _rms(x_ref[...], ng_ref[...]).astype(BF16)
    items.append((300, norm))

    def chunked(dst_ref, off, post):
        for c in range(D_MODEL // CW):
            def fn(c=c):
                dst_ref[slot, :, c * CW:(c + 1) * CW] = post(proj(off + c * CW, CW)).astype(BF16)
            items.append((1024, fn))

    chunked(qa_ref, OFF_QA, lambda v: v * (HEAD_DIM ** -0.5))

    def kv():
        kvf = proj(OFF_KV, 2 * N_KV_HEADS * HEAD_DIM)
        lo = lax.broadcasted_iota(jnp.int32, (TM, LANES), 1) < HEAD_DIM
        for t in range(2):
            xx = kvf[:, t * LANES:(t + 1) * LANES]
            rr = pltpu.roll(xx, HEAD_DIM, 1)
            kv_ref[slot, :, (2 * t) * LANES:(2 * t + 1) * LANES] = jnp.where(lo, xx, rr).astype(BF16)
            kv_ref[slot, :, (2 * t + 1) * LANES:(2 * t + 2) * LANES] = jnp.where(lo, rr, xx).astype(BF16)
    items.append((512, kv))

    chunked(za_ref, OFF_ZA, _silu)

    def halo():
        gext_ref[slot, 0:HALO, :] = halo_fn()
    items.append((10, halo))
    for c in range(D_MODEL // CW):
        def glu(c=c):
            a = proj(OFF_UA + c * CW, CW)
            b = proj(OFF_UB + c * CW, CW)
            gext_ref[slot, HALO:HALO + TM, c * CW:(c + 1) * CW] = a * jax.nn.sigmoid(b)
        items.append((2048, glu))

    chunked(zc_ref, OFF_ZC, _silu)
    chunked(qm_ref, OFF_QM, lambda v: v * (MEM_HEAD_DIM ** -0.5))
    chunked(zm_ref, OFF_ZM, _silu)
    return items


def _mix_items(slot, first, out_rows, sinks_ref, cw_ref, cb_ref, lg_ref, lb_ref, mk_ref, mv_ref, wb_ref,
               ya_ref, yc_ref, ym_ref,
               qa_ref, kv_ref, za_ref, gext_ref, zc_ref, qm_ref, zm_ref,
               bias_ref, kvprev_ref, s_ref, p_ref, y_ref, o_ref):
    items = []
    conv_items = []
    rows_g = GROUP * BLOCK

    def branch_items(n, dst_ref):
        out = []
        for c in range(D_MODEL // CW):
            def fn(c=c):
                cols = slice(c * CW, (c + 1) * CW)
                dst_ref[out_rows:out_rows + TM, cols] = jnp.dot(
                    o_ref[n], wb_ref[n, :, cols], preferred_element_type=F32).astype(BF16)
            out.append((1024, fn))
        return out

    def lo_mask():
        return lax.broadcasted_iota(jnp.int32, (BLOCK, LANES), 1) < HEAD_DIM

    for bi in range(TM // BLOCK):
        rows = slice(bi * BLOCK, (bi + 1) * BLOCK)
        variant = first if bi == 0 else 0
        for g in range(N_KV_HEADS):
            kcols = slice(g * LANES, (g + 1) * LANES)
            vcols = slice((N_KV_HEADS + g) * LANES, (N_KV_HEADS + g + 1) * LANES)

            def band(cols, bi=bi, rows=rows):
                if bi == 0:
                    prev = kvprev_ref[:, cols]
                else:
                    prev = kv_ref[slot, (bi - 1) * BLOCK:bi * BLOCK, cols]
                return jnp.concatenate([prev, kv_ref[slot, rows, cols]], axis=0)

            def scores(g=g, rows=rows, kcols=kcols, band=band):
                lo = lo_mask()
                zero = jnp.zeros((), BF16)
                parts = []
                for j in range(GROUP // 2):
                    pair = g * (GROUP // 2) + j
                    qp = qa_ref[slot, rows, pair * LANES:(pair + 1) * LANES]
                    parts.append(jnp.where(lo, qp, zero))
                    parts.append(jnp.where(lo, zero, qp))
                qs = jnp.concatenate(parts, axis=0)
                s_ref[g] = lax.dot_general(qs, band(kcols), (((1,), (1,)), ((), ())),
                                           preferred_element_type=F32)
            items.append((300, scores))

            for ch in range(rows_g // SWA_CH):
                def softmax(g=g, ch=ch, variant=variant):
                    crow = slice(ch * SWA_CH, (ch + 1) * SWA_CH)
                    sink = sinks_ref[g * GROUP + (ch * SWA_CH) // BLOCK]
                    s = s_ref[g, crow, :] + bias_ref[variant, pl.ds(g * rows_g + ch * SWA_CH, SWA_CH), :]
                    m = jnp.maximum(jnp.max(s, axis=-1, keepdims=True), sink)
                    e = jnp.exp(s - m)
                    denom = jnp.sum(e, axis=-1, keepdims=True) + jnp.exp(sink - m)
                    p_ref[g, crow, :] = (e * (1.0 / denom)).astype(BF16)
                items.append((40, softmax))

            def values(g=g, bi=bi, rows=rows, vcols=vcols, band=band):
                lo = lo_mask()
                o = jnp.dot(p_ref[g], band(vcols), preferred_element_type=F32)
                for j in range(GROUP // 2):
                    pair = g * (GROUP // 2) + j
                    lanes = slice(pair * LANES, (pair + 1) * LANES)
                    oe = o[(2 * j) * BLOCK:(2 * j + 1) * BLOCK]
                    oo = o[(2 * j + 1) * BLOCK:(2 * j + 2) * BLOCK]
                    o_ref[0, rows, lanes] = (jnp.where(lo, oe, oo)
                                             * za_ref[slot, rows, lanes].astype(F32)).astype(BF16)
            items.append((350, values))

    def carry():
        kvprev_ref[...] = kv_ref[slot, TM - BLOCK:TM, :]
    items.append((10, carry))
    items += branch_items(0, ya_ref)

    off = HALO - (CONV_KERNEL - 1)
    win_rows = CONV_R + HALO
    for c in range(D_MODEL // LANES):
        for r in range(TM // CONV_R):
            def conv(c=c, r=r):
                lanes = slice(c * LANES, (c + 1) * LANES)
                win = gext_ref[slot, r * CONV_R:r * CONV_R + win_rows, lanes]
                acc = jnp.zeros((CONV_R, LANES), F32)
                for res in range(SUBLANES):
                    shifted = win if res == 0 else pltpu.roll(win, win_rows - res, 0)
                    for k in range(CONV_KERNEL):
                        if (off + k) % SUBLANES == res:
                            a = off + k - res
                            acc = acc + cw_ref[k:k + 1, lanes] * shifted[a:a + CONV_R]
                y_ref[r * CONV_R:(r + 1) * CONV_R, lanes] = acc + cb_ref[:, lanes]
            conv_items.append((200, conv))

    for r in range(TM // CONV_R):
        def layer_norm(r=r):
            rows = slice(r * CONV_R, (r + 1) * CONV_R)
            y = y_ref[rows, :]
            mu = jnp.mean(y, axis=-1, keepdims=True)
            yc = y - mu
            var = jnp.mean(yc * yc, axis=-1, keepdims=True)
            ln = yc * lax.rsqrt(var + LN_EPS) * lg_ref[...] + lb_ref[...]
            o_ref[1, rows, :] = (_silu(ln) * zc_ref[slot, rows, :].astype(F32)).astype(BF16)
        conv_items.append((400, layer_norm))
    conv_items += branch_items(1, yc_ref)

    for h in range(MEM_HEADS):
        def mem_head(h=h):
            cols = slice(h * MEM_HEAD_DIM, (h + 1) * MEM_HEAD_DIM)
            s = lax.dot_general(qm_ref[slot, :, cols], mk_ref[0, :, cols], (((1,), (1,)), ((), ())),
                                preferred_element_type=F32)
            m = jnp.max(s, axis=-1, keepdims=True)
            e = jnp.exp(s - m)
            p = (e * (1.0 / jnp.sum(e, axis=-1, keepdims=True))).astype(BF16)
            o = jnp.dot(p, mv_ref[0, :, cols], preferred_element_type=F32)
            o_ref[2, :, cols] = (o * zm_ref[slot, :, cols].astype(F32)).astype(BF16)
        items.append((450, mem_head))
    items += branch_items(2, ym_ref)
    return items, conv_items


def _mixers_kernel(sinks_ref, bucket_ref, rbt_ref, x0_ref, xa_ref, xb_ref, ng_ref, w_ref,
                   cw_ref, cb_ref, lg_ref, lb_ref, mk_ref, mv_ref, wb_ref,
                   ya_ref, yc_ref, ym_ref,
                   h_ref, qa_ref, kv_ref, za_ref, gext_ref, zc_ref, qm_ref, zm_ref,
                   bias_ref, kvprev_ref, s_ref, p_ref, y_ref, o_ref, *, tiles_per_seq):
    j = pl.program_id(0)
    stage = (h_ref, qa_ref, kv_ref, za_ref, gext_ref, zc_ref, qm_ref, zm_ref)
    mixargs = (sinks_ref, cw_ref, cb_ref, lg_ref, lb_ref, mk_ref, mv_ref, wb_ref, ya_ref, yc_ref, ym_ref,
               qa_ref, kv_ref, za_ref, gext_ref, zc_ref, qm_ref, zm_ref,
               bias_ref, kvprev_ref, s_ref, p_ref, y_ref, o_ref)

    def zero_halo():
        return jnp.zeros((HALO, D_MODEL), F32)

    @pl.when(j == 0)
    def _():
        bucket = bucket_ref[...]
        rbt = rbt_ref[...]
        r0 = jnp.full((N_HEADS, 2 * BLOCK), NEG, F32)
        for bk in range(NUM_BUCKETS):
            r0 = jnp.where(bucket == bk, rbt[:, bk:bk + 1], r0)
        col = lax.broadcasted_iota(jnp.int32, (BLOCK, 2 * BLOCK), 1)
        for h in range(N_HEADS):
            t = jnp.broadcast_to(r0[h:h + 1, :], (BLOCK, 2 * BLOCK))
            t = pltpu.roll(t, 0, 1, stride=1, stride_axis=0)
            bias_ref[0, h * BLOCK:(h + 1) * BLOCK, :] = t
            bias_ref[1, h * BLOCK:(h + 1) * BLOCK, :] = jnp.where(col < BLOCK, NEG, t)
        kvprev_ref[...] = jnp.zeros(kvprev_ref.shape, BF16)
        for _, fn in _project_items(x0_ref, ng_ref, w_ref, 0, zero_halo, *stage):
            fn()

    steps_per_seq = tiles_per_seq // TILES_PER_STEP
    seq_start = (j % steps_per_seq) == 0
    next_seq_start = (j % steps_per_seq) == steps_per_seq - 1

    def halo_from(slot, zero_if=None):
        def fn():
            tail = gext_ref[slot, TM:TM + HALO, :]
            return tail if zero_if is None else jnp.where(zero_if, zero_halo(), tail)
        return fn

    _interleave(*_mix_items(0, jnp.where(seq_start, 1, 0), 0, *mixargs),
                _project_items(xa_ref, ng_ref, w_ref, 1, halo_from(0), *stage))
    _interleave(*_mix_items(1, 0, TM, *mixargs),
                _project_items(xb_ref, ng_ref, w_ref, 0, halo_from(1, next_seq_start), *stage))


def _mixers(x2, sinks, bucket_row, rbt, ng, w_mix, cw, cb, lg, lb, mk, mv, wb, s):
    t = x2.shape[0]
    n_tiles = t // TM
    tiles_per_seq = s // TM
    steps_per_seq = tiles_per_seq // TILES_PER_STEP
    xspec = lambda k: pl.BlockSpec(
        (TM, D_MODEL), lambda j: (jnp.minimum(TILES_PER_STEP * j + k, n_tiles - 1), 0))
    memspec = pl.BlockSpec((1, MEM_LEN, D_MODEL), lambda j: (j // steps_per_seq, 0, 0))
    ospec = pl.BlockSpec((TILES_PER_STEP * TM, D_MODEL), lambda j: (j, 0))
    return pl.pallas_call(
        functools.partial(_mixers_kernel, tiles_per_seq=tiles_per_seq),
        grid=(n_tiles // TILES_PER_STEP,),
        in_specs=[pl.BlockSpec(memory_space=pltpu.SMEM),
                  _resident((1, 2 * BLOCK)),
                  _resident((N_HEADS, NUM_BUCKETS)),
                  pl.BlockSpec((TM, D_MODEL), lambda j: (0, 0), pipeline_mode=pl.Buffered(1)),
                  xspec(1), xspec(2),
                  _resident((1, D_MODEL)),
                  _resident((D_MODEL, MIX_WIDTH)),
                  _resident((CONV_KERNEL, D_MODEL)),
                  _resident((1, D_MODEL)), _resident((1, D_MODEL)), _resident((1, D_MODEL)),
                  memspec, memspec,
                  _resident((N_BRANCHES, D_MODEL, D_MODEL))],
        out_specs=[ospec, ospec, ospec],
        out_shape=[jax.ShapeDtypeStruct((t, D_MODEL), BF16)] * 3,
        scratch_shapes=[pltpu.VMEM((2, TM, D_MODEL), BF16),
                        pltpu.VMEM((2, TM, D_MODEL), BF16),
                        pltpu.VMEM((2, TM, 4 * LANES), BF16),
                        pltpu.VMEM((2, TM, D_MODEL), BF16),
                        pltpu.VMEM((2, TM + HALO, D_MODEL), F32),
                        pltpu.VMEM((2, TM, D_MODEL), BF16),
                        pltpu.VMEM((2, TM, D_MODEL), BF16),
                        pltpu.VMEM((2, TM, D_MODEL), BF16),
                        pltpu.VMEM((2, N_HEADS * BLOCK, 2 * BLOCK), F32),
                        pltpu.VMEM((BLOCK, 4 * LANES), BF16),
                        pltpu.VMEM((N_KV_HEADS, GROUP * BLOCK, 2 * BLOCK), F32),
                        pltpu.VMEM((N_KV_HEADS, GROUP * BLOCK, 2 * BLOCK), BF16),
                        pltpu.VMEM((TM, D_MODEL), F32),
                        pltpu.VMEM((N_BRANCHES, TM, D_MODEL), BF16)],
        compiler_params=pltpu.CompilerParams(dimension_semantics=("arbitrary",),
                                             vmem_limit_bytes=VMEM_LIMIT),
        name="mixers",
    )(sinks, bucket_row, rbt, x2, x2, x2, ng, w_mix, cw, cb, lg, lb, mk, mv, wb)


MERGE_TM = 256


def _merge_kernel(x_ref, ng_ref, ya_ref, yc_ref, ym_ref, wg_ref, wo_ref, fg_ref, out_ref):
    x = x_ref[...]
    h = _rms(x, ng_ref[...]).astype(BF16)
    y = None
    for n, y_ref in enumerate((ya_ref, yc_ref, ym_ref)):
        gate = jax.nn.sigmoid(jnp.dot(h, wg_ref[:, n * D_MODEL:(n + 1) * D_MODEL], preferred_element_type=F32))
        term = gate * y_ref[...].astype(F32)
        y = term if y is None else y + term
    xo = x + jnp.dot(y.astype(BF16), wo_ref[...], preferred_element_type=F32)
    out_ref[...] = _rms(xo, fg_ref[...])


def _merge(x2, ng, ya, yc, ym, wg, wo, fg):
    t = x2.shape[0]
    tile = pl.BlockSpec((MERGE_TM, D_MODEL), lambda i: (i, 0))
    return pl.pallas_call(
        _merge_kernel,
        grid=(t // MERGE_TM,),
        in_specs=[tile, _resident((1, D_MODEL)), tile, tile, tile,
                  _resident((D_MODEL, N_BRANCHES * D_MODEL)),
                  _resident((D_MODEL, D_MODEL)),
                  _resident((1, D_MODEL))],
        out_specs=tile,
        out_shape=jax.ShapeDtypeStruct((t, D_MODEL), F32),
        compiler_params=pltpu.CompilerParams(dimension_semantics=("arbitrary",),
                                             vmem_limit_bytes=VMEM_LIMIT),
        name="merge",
    )(x2, ng, ya, yc, ym, wg, wo, fg)


def _bucket_row():
    c = jnp.arange(2 * BLOCK)
    d = BLOCK - c
    n = jnp.maximum(d, 0)
    max_exact = NUM_BUCKETS // 2
    nf = jnp.maximum(n, 1).astype(F32)
    large = max_exact + (jnp.log(nf / max_exact) / math.log(MAX_DISTANCE / max_exact)
                         * (NUM_BUCKETS - max_exact)).astype(jnp.int32)
    large = jnp.minimum(large, NUM_BUCKETS - 1)
    bucket = jnp.where(n < max_exact, n, large)
    valid = (d >= 0) & (d < WINDOW)
    return jnp.where(valid, bucket, -1).astype(jnp.int32).reshape(1, 2 * BLOCK)


def kernel(x, mem, rel_bias, norm_g, w_in, sinks, conv_w, conv_b, conv_ln_g, conv_ln_b, mem_norm_g,
           w_mem_kv, w_branch, w_out, final_norm_g):
    nb, s, d = x.shape
    assert d == D_MODEL and w_in.shape[0] == 1 and s % (TILES_PER_STEP * TM) == 0
    x2 = x.reshape(nb * s, d)
    ng = norm_g[0].reshape(1, d)
    w_mix = w_in[0, :, :MIX_WIDTH].astype(BF16)
    w_gate = w_in[0, :, MIX_WIDTH:].astype(BF16)

    mk, mv = _memkv(mem, mem_norm_g[0].reshape(1, d), w_mem_kv[0].astype(BF16))
    ya, yc, ym = _mixers(x2, sinks[0], _bucket_row(), rel_bias.T, ng, w_mix, conv_w[0],
                         conv_b[0].reshape(1, d), conv_ln_g[0].reshape(1, d), conv_ln_b[0].reshape(1, d),
                         mk, mv, w_branch[0].astype(BF16), s)
    out = _merge(x2, ng, ya, yc, ym, w_gate, w_out[0].astype(BF16), final_norm_g.reshape(1, d))
    return out.reshape(nb, s, d)
```
